```python
import math
import jax, jax.numpy as jnp
from jax import lax
import numpy as np

D_MODEL = 2048
BATCH = 4
SEQ = 2048
DEPTH = 2

N_META = 16
CHUNK = 128
PAD = CHUNK - N_META
A_HEAD_DIM = 64
A_V_DIM = 2 * A_HEAD_DIM
A_HEADS = D_MODEL // A_V_DIM
A_QK = A_HEADS * 2 * A_HEAD_DIM
A_VW = A_HEADS * A_V_DIM
R_QK_DIM = 128
R_V_DIM = 2 * R_QK_DIM
R_HEADS = D_MODEL // R_V_DIM
R_QK = R_HEADS * R_QK_DIM
R_VW = R_HEADS * R_V_DIM
D_FF = 128 * ((8 * D_MODEL // 3 + 127) // 128)
CONV_W = 3
ROPE_THETA = 10000.0
ALPHA = (2 * DEPTH) ** 0.25
BETA = (8 * DEPTH) ** -0.25
EPS = 1e-5
NEG = -1e30

kernel_name = "hybrid_diffattn_retention_convffn_deepnorm"


def _split_points():
    sizes = (A_QK, A_QK, A_VW, R_QK, R_QK, R_VW, R_VW, D_MODEL, D_MODEL)
    return tuple(int(s) for s in np.cumsum(sizes)[:-1])


def _w_in_cols():
    return 2 * A_QK + A_VW + 2 * R_QK + 2 * R_VW + 2 * D_MODEL


def layer_norm(x, g, b):
    xf = x.astype(jnp.float32)
    mu = jnp.mean(xf, axis=-1, keepdims=True)
    var = jnp.mean(jnp.square(xf - mu), axis=-1, keepdims=True)
    y = (xf - mu) * lax.rsqrt(var + EPS)
    return (y * g.astype(jnp.float32) + b.astype(jnp.float32)).astype(x.dtype)


def rms_norm(x, g):
    xf = x.astype(jnp.float32)
    y = xf * lax.rsqrt(jnp.mean(jnp.square(xf), axis=-1, keepdims=True) + EPS)
    return (y * g.astype(jnp.float32)).astype(x.dtype)


def head_group_norm(x):
    xf = x.astype(jnp.float32)
    mu = jnp.mean(xf, axis=-1, keepdims=True)
    var = jnp.mean(jnp.square(xf - mu), axis=-1, keepdims=True)
    return ((xf - mu) * lax.rsqrt(var + EPS)).astype(x.dtype)


def rotary(t, pos, inv_freq):
    ang = pos.astype(jnp.float32)[:, None] * inv_freq[None, :]
    ang = jnp.concatenate([ang, ang], axis=-1)
    t1, t2 = jnp.split(t, 2, axis=-1)
    rot = jnp.concatenate([-t2, t1], axis=-1)
    return (t * jnp.cos(ang) + rot * jnp.sin(ang)).astype(t.dtype)


def diff_attention(q1, q2, k1, k2, v, lam):
    B, H, L, d = q1.shape
    nb = L // CHUNK
    scale = d ** -0.5
    kidx = jnp.arange(L)
    key_ok = kidx >= PAD

    def block(args):
        i, qb1, qb2 = args
        qidx = i * CHUNK + jnp.arange(CHUNK)
        allowed = (kidx[None, :] <= qidx[:, None]) & (key_ok[None, :] | (kidx[None, :] == qidx[:, None]))

        def probs(qb, k):
            s = jnp.einsum('bhqd,bhkd->bhqk', qb, k).astype(jnp.float32) * scale
            return jax.nn.softmax(jnp.where(allowed, s, NEG), axis=-1)

        a = probs(qb1, k1) - lam * probs(qb2, k2)
        return jnp.einsum('bhqk,bhkv->bhqv', a.astype(v.dtype), v)

    qb1 = q1.reshape(B, H, nb, CHUNK, d).transpose(2, 0, 1, 3, 4)
    qb2 = q2.reshape(B, H, nb, CHUNK, d).transpose(2, 0, 1, 3, 4)
    o = lax.map(block, (jnp.arange(nb), qb1, qb2))
    return o.transpose(1, 2, 0, 3, 4).reshape(B, H, L, v.shape[-1])


def retention(q, k, v, log_gamma):
    B, H, L, dk = q.shape
    dv = v.shape[-1]
    nc = L // CHUNK
    qc = q.reshape(B, H, nc, CHUNK, dk)
    kc = k.reshape(B, H, nc, CHUNK, dk)
    vc = v.reshape(B, H, nc, CHUNK, dv)
    p = jnp.arange(CHUNK, dtype=jnp.float32)
    diff = p[:, None] - p[None, :]
    lg = log_gamma[:, None, None]
    decay = jnp.where(diff >= 0, jnp.exp(lg * jnp.maximum(diff, 0.0)), 0.0)
    s = jnp.einsum('bhcnk,bhcmk->bhcnm', qc, kc) * decay[None, :, None]
    o_intra = jnp.einsum('bhcnm,bhcmv->bhcnv', s.astype(v.dtype), vc)
    k_tail = kc * jnp.exp(log_gamma[:, None] * (CHUNK - 1 - p))[None, :, None, :, None]
    kv = jnp.einsum('bhcmk,bhcmv->cbhkv', k_tail, vc).astype(jnp.float32)
    chunk_decay = jnp.exp(log_gamma * CHUNK)[None, :, None, None]

    def step(state, kv_c):
        return chunk_decay * state + kv_c, state

    _, prev = lax.scan(step, jnp.zeros((B, H, dk, dv), jnp.float32), kv)
    q_dec = qc * jnp.exp(log_gamma[:, None] * (p + 1.0))[None, :, None, :, None]
    o_cross = jnp.einsum('bhcnk,cbhkv->bhcnv', q_dec, prev)
    return (o_intra + o_cross).astype(v.dtype).reshape(B, H, L, dv)


def hybrid_mixer(h, pos, valid, layer, w_in, lam_q1, lam_k1, lam_q2, lam_k2,
                 subln_g, ret_gn_g, w_branch_a, w_branch_b, w_out):
    B, L, _ = h.shape
    proj = h @ w_in
    aq, ak, av, rq, rk, rv, rg, ga, gb = jnp.split(proj, _split_points(), axis=-1)

    a_freq = 1.0 / (ROPE_THETA ** (jnp.arange(0, A_HEAD_DIM, 2, dtype=jnp.float32) / A_HEAD_DIM))

    def pair_heads(t):
        return t.reshape(B, L, A_HEADS, 2, A_HEAD_DIM).transpose(0, 2, 3, 1, 4)

    qa = rotary(pair_heads(aq), pos, a_freq)
    ka = rotary(pair_heads(ak), pos, a_freq)
    va = av.reshape(B, L, A_HEADS, A_V_DIM).transpose(0, 2, 1, 3)
    lam_init = 0.8 - 0.6 * math.exp(-0.3 * layer)
    lam = (jnp.exp(jnp.sum(lam_q1.astype(jnp.float32) * lam_k1.astype(jnp.float32)))
           - jnp.exp(jnp.sum(lam_q2.astype(jnp.float32) * lam_k2.astype(jnp.float32))) + lam_init)
    o_a = diff_attention(qa[:, :, 0], qa[:, :, 1], ka[:, :, 0], ka[:, :, 1], va, lam)
    o_a = rms_norm(o_a, subln_g) * (1.0 - lam_init)
    branch_a = o_a.transpose(0, 2, 1, 3).reshape(B, L, A_VW) @ w_branch_a

    r_freq = 1.0 / (ROPE_THETA ** jnp.linspace(0.0, 1.0, R_QK_DIM // 2, dtype=jnp.float32))

    def heads(t, dh):
        return t.reshape(B, L, R_HEADS, dh).transpose(0, 2, 1, 3)

    qr = rotary(heads(rq, R_QK_DIM), pos, r_freq)
    kr = rotary(heads(rk, R_QK_DIM), pos, r_freq) * (R_QK_DIM ** -0.5) * valid[None, None, :, None]
    vr = heads(rv, R_V_DIM)
    log_gamma = jnp.log(1.0 - 2.0 ** (-5.0 - jnp.arange(R_HEADS, dtype=jnp.float32)))
    o_r = head_group_norm(retention(qr, kr, vr, log_gamma))
    o_r = o_r.transpose(0, 2, 1, 3).reshape(B, L, R_VW) * ret_gn_g
    branch_b = (o_r * jax.nn.silu(rg)) @ w_branch_b

    merged = jax.nn.sigmoid(ga) * branch_a + jax.nn.sigmoid(gb) * branch_b
    return merged @ w_out


def conv_ffn(h, valid, w_up, conv_w, conv_b, w_down):
    L = h.shape[1]
    u = (h @ w_up) * valid[None, :, None]
    up = jnp.pad(u, ((0, 0), (CONV_W - 1, 0), (0, 0)))
    c = sum(up[:, j:j + L] * conv_w[j] for j in range(CONV_W)) + conv_b
    gate, val = jnp.split(c, 2, axis=-1)
    return (jax.nn.gelu(gate, approximate=False) * val) @ w_down


def setup_inputs(seed: int = 0) -> dict:
    key = jax.random.key(seed)
    ks = jax.random.split(key, 22)
    f32 = jnp.float32
    nrm = lambda k, shape, s: jax.random.normal(k, shape, f32) * s
    cols = _w_in_cols()
    col_scale = jnp.concatenate([
        jnp.ones((2 * A_QK,), f32), jnp.full((A_VW,), BETA, f32),
        jnp.ones((2 * R_QK,), f32), jnp.full((R_VW,), BETA, f32),
        jnp.ones((R_VW + 2 * D_MODEL,), f32)])
    return {
        "x": nrm(ks[0], (BATCH, SEQ, D_MODEL), 1.0),
        "meta_tokens": nrm(ks[1], (N_META, D_MODEL), 1.0),
        "ln_emb_g": 1.0 + nrm(ks[2], (D_MODEL,), 0.02),
        "ln_emb_b": nrm(ks[3], (D_MODEL,), 0.02),
        "w_in": nrm(ks[4], (DEPTH, D_MODEL, cols), D_MODEL ** -0.5) * col_scale,
        "lam_q1": nrm(ks[5], (DEPTH, A_HEAD_DIM), 0.1),
        "lam_k1": nrm(ks[6], (DEPTH, A_HEAD_DIM), 0.1),
        "lam_q2": nrm(ks[7], (DEPTH, A_HEAD_DIM), 0.1),
        "lam_k2": nrm(ks[8], (DEPTH, A_HEAD_DIM), 0.1),
        "subln_g": 1.0 + nrm(ks[9], (DEPTH, A_V_DIM), 0.02),
        "ret_gn_g": 1.0 + nrm(ks[10], (DEPTH, R_VW), 0.02),
        "w_branch_a": nrm(ks[11], (DEPTH, A_VW, D_MODEL), BETA * A_VW ** -0.5),
        "w_branch_b": nrm(ks[12], (DEPTH, R_VW, D_MODEL), BETA * R_VW ** -0.5),
        "w_out": nrm(ks[13], (DEPTH, D_MODEL, D_MODEL), BETA * D_MODEL ** -0.5),
        "ln1_g": 1.0 + nrm(ks[14], (DEPTH, D_MODEL), 0.02),
        "ln1_b": nrm(ks[15], (DEPTH, D_MODEL), 0.02),
        "w_up": nrm(ks[16], (DEPTH, D_MODEL, 2 * D_FF), D_MODEL ** -0.5),
        "conv_w": nrm(ks[17], (DEPTH, CONV_W, 2 * D_FF), CONV_W ** -0.5),
        "conv_b": nrm(ks[18], (DEPTH, 2 * D_FF), 0.02),
        "w_down": nrm(ks[19], (DEPTH, D_FF, D_MODEL), BETA * D_FF ** -0.5),
        "ln2_g": 1.0 + nrm(ks[20], (DEPTH, D_MODEL), 0.02),
        "ln2_b": nrm(ks[21], (DEPTH, D_MODEL), 0.02),
    }


def reference(x, meta_tokens, ln_emb_g, ln_emb_b, w_in, lam_q1, lam_k1, lam_q2, lam_k2,
              subln_g, ret_gn_g, w_branch_a, w_branch_b, w_out, ln1_g, ln1_b,
              w_up, conv_w, conv_b, w_down, ln2_g, ln2_b):
    B = x.shape[0]
    filler = jnp.zeros((B, PAD, D_MODEL), x.dtype)
    meta = jnp.broadcast_to(meta_tokens.astype(x.dtype)[None], (B, N_META, D_MODEL))
    h = jnp.concatenate([filler, meta, x], axis=1)
    L = h.shape[1]
    idx = jnp.arange(L)
    pos = idx - PAD
    valid = (idx >= PAD).astype(x.dtype)
    h = layer_norm(h, ln_emb_g, ln_emb_b)
    for l in range(DEPTH):
        mix = hybrid_mixer(h, pos, valid, l, w_in[l], lam_q1[l], lam_k1[l], lam_q2[l], lam_k2[l],
                           subln_g[l], ret_gn_g[l], w_branch_a[l], w_branch_b[l], w_out[l])
        h = layer_norm(ALPHA * h + mix, ln1_g[l], ln1_b[l])
        ffn = conv_ffn(h, valid, w_up[l], conv_w[l], conv_b[l], w_down[l])
        h = layer_norm(ALPHA * h + ffn, ln2_g[l], ln2_b[l])
    return h[:, PAD + N_META:]
```

```python
import functools
import math

import jax
import jax.numpy as jnp
import numpy as np
from jax import lax
from jax.experimental import pallas as pl
from jax.experimental.pallas import tpu as pltpu

F32 = jnp.float32
BF16 = jnp.bfloat16

D_MODEL = 2048
DEPTH = 2
N_META = 16
CHUNK = 128
PAD = CHUNK - N_META
A_HEAD_DIM = 64
A_V_DIM = 2 * A_HEAD_DIM
A_HEADS = D_MODEL // A_V_DIM
A_QK = A_HEADS * 2 * A_HEAD_DIM
A_VW = A_HEADS * A_V_DIM
R_QK_DIM = 128
R_V_DIM = 2 * R_QK_DIM
R_HEADS = D_MODEL // R_V_DIM
R_QK = R_HEADS * R_QK_DIM
R_VW = R_HEADS * R_V_DIM
D_FF = 128 * ((8 * D_MODEL // 3 + 127) // 128)
CONV_W = 3
ROPE_THETA = 10000.0
ALPHA = (2 * DEPTH) ** 0.25
EPS = 1e-5
NEG = -1e30

LANES = 128
VMEM_LIMIT = 56 * 1024 * 1024

C_AQ = 0
C_AK = C_AQ + A_QK
C_AV = C_AK + A_QK
C_RQ = C_AV + A_VW
C_RK = C_RQ + R_QK
C_RV = C_RK + R_QK
C_RG = C_RV + R_VW
C_GA = C_RG + R_VW
C_GB = C_GA + D_MODEL
W_IN_COLS = C_GB + D_MODEL

TM_BIG = 1088
TM_SMALL = 544
TN_PROJ = 1024
TN_FF = 512
D_FF_PAD = TN_FF * pl.cdiv(D_FF, TN_FF)


def _params(*sem):
    return pltpu.CompilerParams(dimension_semantics=sem, vmem_limit_bytes=VMEM_LIMIT)


def _layer_norm_rows(z, g, b):
    mu = jnp.mean(z, axis=-1, keepdims=True)
    d = z - mu
    var = jnp.mean(d * d, axis=-1, keepdims=True)
    return d * lax.rsqrt(var + EPS) * g + b


def _sigmoid(x):
    return 1.0 / (1.0 + jnp.exp(-x))


def _embed_ln_kernel(x_ref, meta_ref, g_ref, b_ref, hf_ref, hb_ref):
    i = pl.program_id(1)
    g = g_ref[...]
    b = b_ref[...]

    @pl.when(i == 0)
    def _():
        rows = jnp.concatenate([jnp.zeros((PAD, D_MODEL), F32), meta_ref[...]], axis=0)
        y = _layer_norm_rows(rows, g, b)
        hf_ref[...] = y
        hb_ref[...] = y.astype(BF16)

    @pl.when(i > 0)
    def _():
        y = _layer_norm_rows(x_ref[...], g, b)
        hf_ref[...] = y
        hb_ref[...] = y.astype(BF16)


def _embed_ln(x, meta, g, b, nb):
    B = x.shape[0]
    T = B * nb * CHUNK
    return pl.pallas_call(
        _embed_ln_kernel,
        out_shape=(jax.ShapeDtypeStruct((T, D_MODEL), F32), jax.ShapeDtypeStruct((T, D_MODEL), BF16)),
        grid=(B, nb),
        in_specs=[
            pl.BlockSpec((None, CHUNK, D_MODEL), lambda bi, i: (bi, jnp.maximum(i - 1, 0), 0)),
            pl.BlockSpec((N_META, D_MODEL), lambda bi, i: (0, 0)),
            pl.BlockSpec((1, D_MODEL), lambda bi, i: (0, 0)),
            pl.BlockSpec((1, D_MODEL), lambda bi, i: (0, 0)),
        ],
        out_specs=(
            pl.BlockSpec((CHUNK, D_MODEL), lambda bi, i: (bi * nb + i, 0)),
            pl.BlockSpec((CHUNK, D_MODEL), lambda bi, i: (bi * nb + i, 0)),
        ),
        compiler_params=_params("arbitrary", "arbitrary"),
        name="embed_ln",
    )(x, meta, g.reshape(1, D_MODEL), b.reshape(1, D_MODEL))


def _proj_kernel(kind, h_ref, w_ref, *rest):
    if kind == "plain":
        o_ref, wb_ref = rest
    elif kind == "rot_a":
        cos_ref, sin_ref, o_ref, wb_ref = rest
    else:
        cos_ref, sin_ref, post_ref, o_ref, wb_ref = rest

    @pl.when(pl.program_id(1) == 0)
    def _():
        wb_ref[...] = w_ref[...].astype(BF16)

    acc = jnp.dot(h_ref[...], wb_ref[...], preferred_element_type=F32)
    tm, tn = acc.shape
    if kind == "plain":
        o_ref[...] = acc.astype(o_ref.dtype)
        return
    cos = cos_ref[...]
    sin = sin_ref[...]
    if kind == "rot_a":
        lane = lax.broadcasted_iota(jnp.int32, (tm, LANES), 1)
        low_half = (lane & (A_HEAD_DIM // 2)) == 0
    else:
        post = post_ref[...]
    for j in range(tn // LANES):
        t = acc[:, j * LANES:(j + 1) * LANES]
        if kind == "rot_a":
            rot = jnp.where(low_half, pltpu.roll(t, LANES - A_HEAD_DIM // 2, 1),
                            pltpu.roll(t, A_HEAD_DIM // 2, 1))
            y = t * cos + rot * sin
        else:
            y = (t * cos + pltpu.roll(t, R_QK_DIM // 2, 1) * sin) * post
        o_ref[:, j * LANES:(j + 1) * LANES] = y.astype(o_ref.dtype)


def _proj(hb, w_in, layer, col0, ncols, kind, out_dtype, tables=(), variant=None):
    T = hb.shape[0]
    tm, tn = TM_BIG, TN_PROJ
    nt, mt = ncols // tn, T // tm
    tile0 = col0 // tn
    in_specs = [
        pl.BlockSpec((tm, D_MODEL), lambda n, m: (m, 0)),
        pl.BlockSpec((None, D_MODEL, tn), lambda n, m: (layer, 0, tile0 + n)),
    ]
    for _ in tables:
        in_specs.append(pl.BlockSpec((None, tm, LANES), lambda n, m: (variant(n), m % 2, 0)))
    return pl.pallas_call(
        functools.partial(_proj_kernel, kind),
        out_shape=jax.ShapeDtypeStruct((T, ncols), out_dtype),
        grid=(nt, mt),
        in_specs=in_specs,
        out_specs=pl.BlockSpec((tm, tn), lambda n, m: (m, n)),
        scratch_shapes=[pltpu.VMEM((D_MODEL, tn), BF16)],
        compiler_params=_params("arbitrary", "arbitrary"),
        name="proj_" + kind,
    )(hb, w_in, *tables)


def _attn_kernel(lam_init, q_blocks, q_ref, k_ref, v_ref, lamp_ref, g_ref, o_ref):
    lp = lamp_ref[...]
    lam = (jnp.exp(jnp.sum(lp[0:1] * lp[1:2], axis=-1, keepdims=True))
           - jnp.exp(jnp.sum(lp[2:3] * lp[3:4], axis=-1, keepdims=True)) + lam_init)
    lane = lax.broadcasted_iota(jnp.int32, (1, LANES), 1)
    map1 = jnp.where(lane < A_HEAD_DIM, 1.0, 0.0).astype(BF16)
    map2 = jnp.where(lane >= A_HEAD_DIM, 1.0, 0.0).astype(BF16)
    g = g_ref[...]
    nt = (((1,), (1,)), ((), ()))

    for r0, r1 in q_blocks:
        tq, keys = r1 - r0, r1
        q = q_ref[r0:r1, :]
        k = k_ref[0:keys, :]
        qq = r0 + lax.broadcasted_iota(jnp.int32, (tq, CHUNK), 0)
        kk = lax.broadcasted_iota(jnp.int32, (tq, CHUNK), 1)

        def masked_probs(qm):
            s = lax.dot_general(qm, k, nt, preferred_element_type=F32)
            parts = []
            for c0 in range(0, keys, CHUNK):
                piece = s[:, c0:c0 + CHUNK]
                cond = None
                if c0 == 0:
                    cond = kk >= PAD
                    if r0 == 0:
                        cond = cond | (kk == qq)
                if c0 + CHUNK > r0:
                    causal = (kk + c0) <= qq
                    cond = causal if cond is None else (cond & causal)
                if cond is not None:
                    piece = jnp.where(cond, piece, NEG)
                parts.append(piece)
            s = parts[0] if len(parts) == 1 else jnp.concatenate(parts, axis=1)
            p = jnp.exp(s - jnp.max(s, axis=-1, keepdims=True))
            return p, jnp.sum(p, axis=-1, keepdims=True)

        p1, l1 = masked_probs(q * map1)
        p2, l2 = masked_probs(q * map2)
        a = p1 * (1.0 / l1) - p2 * (lam * (1.0 / l2))
        o = jnp.dot(a.astype(BF16), v_ref[0:keys, :], preferred_element_type=F32)
        y = o * lax.rsqrt(jnp.mean(o * o, axis=-1, keepdims=True) + EPS) * g
        o_ref[r0:r1, :] = (y * (1.0 - lam_init)).astype(BF16)


def _attention(qk, v, lamp, subln_g, layer, B, L):
    T = B * L
    lam_init = 0.8 - 0.6 * math.exp(-0.3 * layer)
    q_blocks = tuple((r, r + CHUNK) for r in range(0, L, CHUNK))
    kh = A_QK // LANES
    return pl.pallas_call(
        functools.partial(_attn_kernel, lam_init, q_blocks),
        out_shape=jax.ShapeDtypeStruct((T, A_VW), BF16),
        grid=(B, A_HEADS),
        in_specs=[
            pl.BlockSpec((L, LANES), lambda b, h: (b, h)),
            pl.BlockSpec((L, LANES), lambda b, h: (b, kh + h)),
            pl.BlockSpec((L, A_V_DIM), lambda b, h: (b, h)),
            pl.BlockSpec((4, A_HEAD_DIM), lambda b, h: (0, 0)),
            pl.BlockSpec((1, A_V_DIM), lambda b, h: (0, 0)),
        ],
        out_specs=pl.BlockSpec((L, A_V_DIM), lambda b, h: (b, h)),
        compiler_params=_params("arbitrary", "arbitrary"),
        name="diff_attention",
    )(qk, qk, v, lamp, subln_g.reshape(1, A_V_DIM))


def _retention_kernel(nc, q_ref, k_ref, v_ref, rg_ref, gn_ref, decay_ref, tail_ref, qdec_ref,
                      cdec_ref, o_ref):
    decay = decay_ref[...]
    tail = tail_ref[...]
    qdec = qdec_ref[...]
    cdec = cdec_ref[...]
    gn = gn_ref[...]
    nt = (((1,), (1,)), ((), ()))
    tn = (((0,), (0,)), ((), ()))
    state = None
    for c in range(nc):
        rows = slice(c * CHUNK, (c + 1) * CHUNK)
        qc = q_ref[rows, :]
        kc = k_ref[rows, :]
        vc = v_ref[rows, :]
        s = lax.dot_general(qc.astype(BF16), kc.astype(BF16), nt, preferred_element_type=F32) * decay
        o = jnp.dot(s.astype(BF16), vc, preferred_element_type=F32)
        if state is not None:
            o = o + jnp.dot((qc * qdec).astype(BF16), state.astype(BF16), preferred_element_type=F32)
        if c + 1 < nc:
            kv = lax.dot_general((kc * tail).astype(BF16), vc, tn, preferred_element_type=F32)
            state = kv if state is None else cdec * state + kv
        mu = jnp.mean(o, axis=-1, keepdims=True)
        d = o - mu
        var = jnp.mean(d * d, axis=-1, keepdims=True)
        y = d * lax.rsqrt(var + EPS) * gn
        rg = rg_ref[rows, :]
        o_ref[rows, :] = (y * (rg * _sigmoid(rg))).astype(BF16)


def _retention(rqk, rv, gates, gn_g, tabs, B, L):
    T = B * L
    decay, tail, qdec, cdec = tabs
    kh = R_QK // R_QK_DIM
    return pl.pallas_call(
        functools.partial(_retention_kernel, L // CHUNK),
        out_shape=jax.ShapeDtypeStruct((T, R_VW), BF16),
        grid=(B, R_HEADS),
        in_specs=[
            pl.BlockSpec((L, R_QK_DIM), lambda b, h: (b, h)),
            pl.BlockSpec((L, R_QK_DIM), lambda b, h: (b, kh + h)),
            pl.BlockSpec((L, R_V_DIM), lambda b, h: (b, h)),
            pl.BlockSpec((L, R_V_DIM), lambda b, h: (b, h)),
            pl.BlockSpec((1, R_V_DIM), lambda b, h: (0, h)),
            pl.BlockSpec((None, CHUNK, CHUNK), lambda b, h: (h, 0, 0)),
            pl.BlockSpec((None, CHUNK, R_QK_DIM), lambda b, h: (h, 0, 0)),
            pl.BlockSpec((None, CHUNK, R_QK_DIM), lambda b, h: (h, 0, 0)),
            pl.BlockSpec((None, 1, R_V_DIM), lambda b, h: (h, 0, 0)),
        ],
        out_specs=pl.BlockSpec((L, R_V_DIM), lambda b, h: (b, h)),
        compiler_params=_params("arbitrary", "arbitrary"),
        name="retention",
    )(rqk, rqk, rv, gates, gn_g.reshape(1, R_VW), decay, tail, qdec, cdec)


def _merge_kernel(oa_ref, ob_ref, wa_ref, wb_ref, ga_ref, gb_ref, o_ref, was_ref, wbs_ref):
    @pl.when(pl.program_id(1) == 0)
    def _():
        was_ref[...] = wa_ref[...].astype(BF16)
        wbs_ref[...] = wb_ref[...].astype(BF16)

    ba = jnp.dot(oa_ref[...], was_ref[...], preferred_element_type=F32)
    bb = jnp.dot(ob_ref[...], wbs_ref[...], preferred_element_type=F32)
    o_ref[...] = (_sigmoid(ga_ref[...]) * ba + _sigmoid(gb_ref[...]) * bb).astype(BF16)


def _merge(oa, ob, w_a, w_b, gates, layer):
    T = oa.shape[0]
    tm, tn = TM_SMALL, TN_FF
    ga0 = (C_GA - C_RG) // tn
    gb0 = (C_GB - C_RG) // tn
    return pl.pallas_call(
        _merge_kernel,
        out_shape=jax.ShapeDtypeStruct((T, D_MODEL), BF16),
        grid=(D_MODEL // tn, T // tm),
        in_specs=[
            pl.BlockSpec((tm, A_VW), lambda n, m: (m, 0)),
            pl.BlockSpec((tm, R_VW), lambda n, m: (m, 0)),
            pl.BlockSpec((None, A_VW, tn), lambda n, m: (layer, 0, n)),
            pl.BlockSpec((None, R_VW, tn), lambda n, m: (layer, 0, n)),
            pl.BlockSpec((tm, tn), lambda n, m: (m, ga0 + n)),
            pl.BlockSpec((tm, tn), lambda n, m: (m, gb0 + n)),
        ],
        out_specs=pl.BlockSpec((tm, tn), lambda n, m: (m, n)),
        scratch_shapes=[pltpu.VMEM((A_VW, tn), BF16), pltpu.VMEM((R_VW, tn), BF16)],
        compiler_params=_params("arbitrary", "arbitrary"),
        name="merge",
    )(oa, ob, w_a, w_b, gates, gates)


def _proj_ln_kernel(tn, a_ref, w_ref, h_ref, g_ref, b_ref, of_ref, ob_ref, z_ref):
    n = pl.program_id(1)
    acc = jnp.dot(a_ref[...], w_ref[...].astype(BF16), preferred_element_type=F32)
    col = pl.multiple_of(n * tn, tn)
    z_ref[:, pl.ds(col, tn)] = ALPHA * h_ref[...] + acc

    @pl.when(n == pl.num_programs(1) - 1)
    def _():
        y = _layer_norm_rows(z_ref[...], g_ref[...], b_ref[...])
        of_ref[...] = y
        ob_ref[...] = y.astype(BF16)


def _proj_ln(a, kdim, w, layer, hf, g, b, tn, name):
    T = hf.shape[0]
    tm = TM_SMALL
    return pl.pallas_call(
        functools.partial(_proj_ln_kernel, tn),
        out_shape=(jax.ShapeDtypeStruct((T, D_MODEL), F32), jax.ShapeDtypeStruct((T, D_MODEL), BF16)),
        grid=(T // tm, D_MODEL // tn),
        in_specs=[
            pl.BlockSpec((tm, kdim), lambda m, n: (m, 0)),
            pl.BlockSpec((None, kdim, tn), lambda m, n: (layer, 0, n)),
            pl.BlockSpec((tm, tn), lambda m, n: (m, n)),
            pl.BlockSpec((None, 1, D_MODEL), lambda m, n: (layer, 0, 0)),
            pl.BlockSpec((None, 1, D_MODEL), lambda m, n: (layer, 0, 0)),
        ],
        out_specs=(
            pl.BlockSpec((tm, D_MODEL), lambda m, n: (m, 0)),
            pl.BlockSpec((tm, D_MODEL), lambda m, n: (m, 0)),
        ),
        scratch_shapes=[pltpu.VMEM((tm, D_MODEL), F32)],
        compiler_params=_params("arbitrary", "arbitrary"),
        name=name,
    )(a, w, hf, g.reshape(DEPTH, 1, D_MODEL), b.reshape(DEPTH, 1, D_MODEL))


def _ffn_up_kernel(h_ref, wg_ref, wv_ref, cwg_ref, cwv_ref, cbg_ref, cbv_ref, o_ref,
                   wgs_ref, wvs_ref, carry_g_ref, carry_v_ref):
    m = pl.program_id(1)

    @pl.when(m == 0)
    def _():
        wgs_ref[...] = wg_ref[...].astype(BF16)
        wvs_ref[...] = wv_ref[...].astype(BF16)

    seq_start = (m % 2) == 0
    x = h_ref[...]
    tm = x.shape[0]
    tn = o_ref.shape[1]
    row = lax.broadcasted_iota(jnp.int32, (tm, tn), 0)
    keep = jnp.logical_or(jnp.logical_not(seq_start), row >= PAD)

    @pl.when(seq_start)
    def _():
        carry_g_ref[...] = jnp.zeros_like(carry_g_ref)
        carry_v_ref[...] = jnp.zeros_like(carry_v_ref)

    def conv_half(ws_ref, cw_ref, cb_ref, carry_ref):
        u = jnp.dot(x, ws_ref[...], preferred_element_type=F32)
        u = jnp.where(keep, u, 0.0)
        prev = carry_ref[...]
        carry_ref[...] = u[tm - 8:tm, :]
        u1 = jnp.where(row == 0, prev[7:8], pltpu.roll(u, 1, 0))
        u2 = jnp.where(row == 0, prev[6:7], jnp.where(row == 1, prev[7:8], pltpu.roll(u, 2, 0)))
        cw = cw_ref[...]
        return u2 * cw[0:1] + u1 * cw[1:2] + u * cw[2:3] + cb_ref[...]

    gate = conv_half(wgs_ref, cwg_ref, cbg_ref, carry_g_ref)
    val = conv_half(wvs_ref, cwv_ref, cbv_ref, carry_v_ref)
    gelu = 0.5 * gate * (1.0 + lax.erf(gate * (2.0 ** -0.5)))
    o_ref[...] = (gelu * val).astype(BF16)


def _ffn_up(hb, w_up, w_val, conv_w, conv_w_val, conv_b, conv_b_val, layer):
    T = hb.shape[0]
    tm, tn = TM_BIG, TN_FF
    return pl.pallas_call(
        _ffn_up_kernel,
        out_shape=jax.ShapeDtypeStruct((T, D_FF_PAD), BF16),
        grid=(D_FF_PAD // tn, T // tm),
        in_specs=[
            pl.BlockSpec((tm, D_MODEL), lambda n, m: (m, 0)),
            pl.BlockSpec((None, D_MODEL, tn), lambda n, m: (layer, 0, n)),
            pl.BlockSpec((None, D_MODEL, tn), lambda n, m: (layer, 0, n)),
            pl.BlockSpec((None, CONV_W, tn), lambda n, m: (layer, 0, n)),
            pl.BlockSpec((None, CONV_W, tn), lambda n, m: (layer, 0, n)),
            pl.BlockSpec((None, 1, tn), lambda n, m: (layer, 0, n)),
            pl.BlockSpec((None, 1, tn), lambda n, m: (layer, 0, n)),
        ],
        out_specs=pl.BlockSpec((tm, tn), lambda n, m: (m, n)),
        scratch_shapes=[pltpu.VMEM((D_MODEL, tn), BF16), pltpu.VMEM((D_MODEL, tn), BF16),
                        pltpu.VMEM((8, tn), F32), pltpu.VMEM((8, tn), F32)],
        compiler_params=_params("arbitrary", "arbitrary"),
        name="ffn_up_conv_geglu",
    )(hb, w_up, w_val, conv_w, conv_w_val, conv_b, conv_b_val)


def _rotary_tables(L):
    pos = (jnp.arange(L) - PAD).astype(F32)
    a_freq = 1.0 / (ROPE_THETA ** (jnp.arange(0, A_HEAD_DIM, 2, dtype=F32) / A_HEAD_DIM))
    ang = pos[:, None] * a_freq[None, :]
    ang = jnp.concatenate([ang] * (LANES // (A_HEAD_DIM // 2)), axis=-1)
    lane = jnp.arange(LANES)
    sign_a = jnp.where((lane & (A_HEAD_DIM // 2)) == 0, -1.0, 1.0).astype(F32)
    cos_a, sin_a = jnp.cos(ang), jnp.sin(ang) * sign_a
    scale = A_HEAD_DIM ** -0.5
    cos_a = jnp.stack([cos_a * scale, cos_a])
    sin_a = jnp.stack([sin_a * scale, sin_a])

    r_freq = 1.0 / (ROPE_THETA ** jnp.linspace(0.0, 1.0, R_QK_DIM // 2, dtype=F32))
    ang = pos[:, None] * r_freq[None, :]
    ang = jnp.concatenate([ang, ang], axis=-1)
    sign_r = jnp.where(lane < R_QK_DIM // 2, -1.0, 1.0).astype(F32)
    cos_r = jnp.broadcast_to(jnp.cos(ang), (2, L, LANES))
    sin_r = jnp.broadcast_to(jnp.sin(ang) * sign_r, (2, L, LANES))
    valid = (jnp.arange(L) >= PAD).astype(F32)
    post_k = jnp.broadcast_to(((R_QK_DIM ** -0.5) * valid)[:, None], (L, LANES))
    post_r = jnp.stack([jnp.ones((L, LANES), F32), post_k])
    return (cos_a, sin_a), (cos_r, sin_r, post_r)


def _retention_tables():
    log_gamma = jnp.log(1.0 - 2.0 ** (-5.0 - jnp.arange(R_HEADS, dtype=F32)))
    p = jnp.arange(CHUNK, dtype=F32)
    diff = p[:, None] - p[None, :]
    lg = log_gamma[:, None, None]
    decay = jnp.where(diff >= 0, jnp.exp(lg * jnp.maximum(diff, 0.0)), 0.0)
    tail = jnp.exp(log_gamma[:, None] * (CHUNK - 1 - p))
    qdec = jnp.exp(log_gamma[:, None] * (p + 1.0))
    tail = jnp.broadcast_to(tail[:, :, None], (R_HEADS, CHUNK, R_QK_DIM))
    qdec = jnp.broadcast_to(qdec[:, :, None], (R_HEADS, CHUNK, R_QK_DIM))
    cdec = jnp.broadcast_to(jnp.exp(log_gamma * CHUNK)[:, None, None], (R_HEADS, 1, R_V_DIM))
    return decay, tail, qdec, cdec


def kernel(x, meta_tokens, ln_emb_g, ln_emb_b, w_in, lam_q1, lam_k1, lam_q2, lam_k2, subln_g, ret_gn_g,
           w_branch_a, w_branch_b, w_out, ln1_g, ln1_b, w_up, conv_w, conv_b, w_down, ln2_g, ln2_b):
    B, seq, _ = x.shape
    nb = 1 + seq // CHUNK
    L = nb * CHUNK
    rot_a, rot_r = _rotary_tables(L)
    ret_tabs = _retention_tables()

    padc = D_FF_PAD - D_FF
    w_val = jnp.pad(w_up[:, :, D_FF:], ((0, 0), (0, 0), (0, padc)))
    conv_w_val = jnp.pad(conv_w[:, :, D_FF:], ((0, 0), (0, 0), (0, padc)))
    conv_b3 = conv_b.reshape(DEPTH, 1, 2 * D_FF)
    conv_b_val = jnp.pad(conv_b3[:, :, D_FF:], ((0, 0), (0, 0), (0, padc)))

    hf, hb = _embed_ln(x, meta_tokens, ln_emb_g, ln_emb_b, nb)
    for l in range(DEPTH):
        qk = _proj(hb, w_in, l, C_AQ, 2 * A_QK, "rot_a", BF16, rot_a, lambda n: n // (A_QK // TN_PROJ))
        av = _proj(hb, w_in, l, C_AV, A_VW, "plain", BF16)
        rqk = _proj(hb, w_in, l, C_RQ, 2 * R_QK, "rot_r", F32, rot_r, lambda n: n // (R_QK // TN_PROJ))
        rv = _proj(hb, w_in, l, C_RV, R_VW, "plain", BF16)
        gates = _proj(hb, w_in, l, C_RG, R_VW + 2 * D_MODEL, "plain", F32)

        lamp = jnp.stack([lam_q1[l], lam_k1[l], lam_q2[l], lam_k2[l]])
        oa = _attention(qk, av, lamp, subln_g[l], l, B, L)
        ob = _retention(rqk, rv, gates, ret_gn_g[l], ret_tabs, B, L)
        merged = _merge(oa, ob, w_branch_a, w_branch_b, gates, l)
        hf, hb = _proj_ln(merged, D_MODEL, w_out, l, hf, ln1_g, ln1_b, TN_FF, "out_proj_ln")

        gact = _ffn_up(hb, w_up, w_val, conv_w, conv_w_val, conv_b3, conv_b_val, l)
        hf, hb = _proj_ln(gact, D_FF, w_down, l, hf, ln2_g, ln2_b, 256, "ffn_down_ln")
    return hf.reshape(B, L, D_MODEL)[:, CHUNK:]
```

```python
import functools
import math

import jax
import jax.numpy as jnp
import numpy as np
from jax import lax
from jax.experimental import pallas as pl
from jax.experimental.pallas import tpu as pltpu

F32 = jnp.float32
BF16 = jnp.bfloat16

D_MODEL = 2048
DEPTH = 2
N_META = 16
CHUNK = 128
PAD = CHUNK - N_META
A_HEAD_DIM = 64
A_V_DIM = 2 * A_HEAD_DIM
A_HEADS = D_MODEL // A_V_DIM
A_QK = A_HEADS * 2 * A_HEAD_DIM
A_VW = A_HEADS * A_V_DIM
R_QK_DIM = 128
R_V_DIM = 2 * R_QK_DIM
R_HEADS = D_MODEL // R_V_DIM
R_QK = R_HEADS * R_QK_DIM
R_VW = R_HEADS * R_V_DIM
D_FF = 128 * ((8 * D_MODEL // 3 + 127) // 128)
CONV_W = 3
ROPE_THETA = 10000.0
ALPHA = (2 * DEPTH) ** 0.25
EPS = 1e-5
NEG = -1e30

LANES = 128
VMEM_LIMIT = 56 * 1024 * 1024

C_AQ = 0
C_AK = C_AQ + A_QK
C_AV = C_AK + A_QK
C_RQ = C_AV + A_VW
C_RK = C_RQ + R_QK
C_RV = C_RK + R_QK
C_RG = C_RV + R_VW
C_GA = C_RG + R_VW
C_GB = C_GA + D_MODEL
W_IN_COLS = C_GB + D_MODEL

TM_BIG = 1088
TM_SMALL = 544
TN_PROJ = 1024
TN_FF = 512
D_FF_PAD = TN_FF * pl.cdiv(D_FF, TN_FF)
TQ = 256
ROW_SPLIT = 4
HALO = 8


def _params(*sem):
    return pltpu.CompilerParams(dimension_semantics=sem, vmem_limit_bytes=VMEM_LIMIT)


def _layer_norm_rows(z, g, b):
    mu = jnp.mean(z, axis=-1, keepdims=True)
    d = z - mu
    var = jnp.mean(d * d, axis=-1, keepdims=True)
    return d * lax.rsqrt(var + EPS) * g + b


def _sigmoid(x):
    return 1.0 / (1.0 + jnp.exp(-x))


def _software_pipeline(n, produce, consume):
    nxt = produce(0)
    for i in range(n):
        cur = nxt
        if i + 1 < n:
            nxt = produce(i + 1)
        consume(i, cur)


def _embed_ln_kernel(x_ref, meta_ref, g_ref, b_ref, hf_ref, hb_ref):
    i = pl.program_id(1)
    g = g_ref[...]
    b = b_ref[...]

    @pl.when(i == 0)
    def _():
        rows = jnp.concatenate([jnp.zeros((PAD, D_MODEL), F32), meta_ref[...]], axis=0)
        y = _layer_norm_rows(rows, g, b)
        hf_ref[...] = y
        hb_ref[...] = y.astype(BF16)

    @pl.when(i > 0)
    def _():
        y = _layer_norm_rows(x_ref[...], g, b)
        hf_ref[...] = y
        hb_ref[...] = y.astype(BF16)


def _embed_ln(x, meta, g, b, nb):
    B = x.shape[0]
    T = B * nb * CHUNK
    return pl.pallas_call(
        _embed_ln_kernel,
        out_shape=(jax.ShapeDtypeStruct((T, D_MODEL), F32), jax.ShapeDtypeStruct((T, D_MODEL), BF16)),
        grid=(B, nb),
        in_specs=[
            pl.BlockSpec((None, CHUNK, D_MODEL), lambda bi, i: (bi, jnp.maximum(i - 1, 0), 0)),
            pl.BlockSpec((N_META, D_MODEL), lambda bi, i: (0, 0)),
            pl.BlockSpec((1, D_MODEL), lambda bi, i: (0, 0)),
            pl.BlockSpec((1, D_MODEL), lambda bi, i: (0, 0)),
        ],
        out_specs=(
            pl.BlockSpec((CHUNK, D_MODEL), lambda bi, i: (bi * nb + i, 0)),
            pl.BlockSpec((CHUNK, D_MODEL), lambda bi, i: (bi * nb + i, 0)),
        ),
        compiler_params=_params("arbitrary", "arbitrary"),
        name="embed_ln",
    )(x, meta, g.reshape(1, D_MODEL), b.reshape(1, D_MODEL))


def _proj_kernel(kind, h_ref, w_ref, *rest):
    if kind == "plain":
        o_ref, wb_ref = rest
    elif kind == "rot_a":
        cos_ref, sin_ref, o_ref, wb_ref = rest
    else:
        cos_ref, sin_ref, post_ref, o_ref, wb_ref = rest

    @pl.when(pl.program_id(1) == 0)
    def _():
        wb_ref[...] = w_ref[...].astype(BF16)

    tm, tn = o_ref.shape
    rb = tm // ROW_SPLIT
    if kind == "rot_a":
        lane = lax.broadcasted_iota(jnp.int32, (rb, LANES), 1)
        low_half = (lane & (A_HEAD_DIM // 2)) == 0

    def matmul(r):
        return jnp.dot(h_ref[r * rb:(r + 1) * rb, :], wb_ref[...], preferred_element_type=F32)

    def epilogue(r, acc):
        rows = slice(r * rb, (r + 1) * rb)
        if kind == "plain":
            o_ref[rows, :] = acc.astype(o_ref.dtype)
            return
        cos = cos_ref[rows, :]
        sin = sin_ref[rows, :]
        for j in range(tn // LANES):
            t = acc[:, j * LANES:(j + 1) * LANES]
            if kind == "rot_a":
                rot = jnp.where(low_half, pltpu.roll(t, LANES - A_HEAD_DIM // 2, 1),
                                pltpu.roll(t, A_HEAD_DIM // 2, 1))
                y = t * cos + rot * sin
            else:
                y = (t * cos + pltpu.roll(t, R_QK_DIM // 2, 1) * sin) * post_ref[rows, :]
            o_ref[rows, j * LANES:(j + 1) * LANES] = y.astype(o_ref.dtype)

    _software_pipeline(ROW_SPLIT, matmul, epilogue)


def _proj(hb, w_in, layer, col0, ncols, kind, out_dtype, tables=(), variant=None):
    T = hb.shape[0]
    tm, tn = TM_BIG, TN_PROJ
    nt, mt = ncols // tn, T // tm
    tile0 = col0 // tn
    in_specs = [
        pl.BlockSpec((tm, D_MODEL), lambda n, m: (m, 0)),
        pl.BlockSpec((None, D_MODEL, tn), lambda n, m: (layer, 0, tile0 + n)),
    ]
    for _ in tables:
        in_specs.append(pl.BlockSpec((None, tm, LANES), lambda n, m: (variant(n), m % 2, 0)))
    return pl.pallas_call(
        functools.partial(_proj_kernel, kind),
        out_shape=jax.ShapeDtypeStruct((T, ncols), out_dtype),
        grid=(nt, mt),
        in_specs=in_specs,
        out_specs=pl.BlockSpec((tm, tn), lambda n, m: (m, n)),
        scratch_shapes=[pltpu.VMEM((D_MODEL, tn), BF16)],
        compiler_params=_params("arbitrary", "arbitrary"),
        name="proj_" + kind,
    )(hb, w_in, *tables)


def _attn_kernel(lam_init, q_blocks, q_ref, k_ref, v_ref, lamp_ref, g_ref, o_ref):
    lp = lamp_ref[...]
    lam = (jnp.exp(jnp.sum(lp[0:1] * lp[1:2], axis=-1, keepdims=True))
           - jnp.exp(jnp.sum(lp[2:3] * lp[3:4], axis=-1, keepdims=True)) + lam_init)
    lane = lax.broadcasted_iota(jnp.int32, (1, LANES), 1)
    map1 = jnp.where(lane < A_HEAD_DIM, 1.0, 0.0).astype(BF16)
    map2 = jnp.where(lane >= A_HEAD_DIM, 1.0, 0.0).astype(BF16)
    g = g_ref[...]
    nt = (((1,), (1,)), ((), ()))

    def scores(r0, r1):
        q = q_ref[r0:r1, :]
        return lax.dot_general(jnp.concatenate([q * map1, q * map2], axis=0), k_ref[0:r1, :], nt,
                               preferred_element_type=F32)

    def finish(r0, r1, s12):
        tq, keys = r1 - r0, r1
        qq = r0 + lax.broadcasted_iota(jnp.int32, (tq, CHUNK), 0)
        kk = lax.broadcasted_iota(jnp.int32, (tq, CHUNK), 1)

        def masked_probs(s):
            parts = []
            for c0 in range(0, keys, CHUNK):
                piece = s[:, c0:c0 + CHUNK]
                cond = None
                if c0 == 0:
                    cond = kk >= PAD
                    if r0 == 0:
                        cond = cond | (kk == qq)
                if c0 + CHUNK > r0:
                    causal = (kk + c0) <= qq
                    cond = causal if cond is None else (cond & causal)
                if cond is not None:
                    piece = jnp.where(cond, piece, NEG)
                parts.append(piece)
            s = parts[0] if len(parts) == 1 else jnp.concatenate(parts, axis=1)
            p = jnp.exp2(s - jnp.max(s, axis=-1, keepdims=True))
            return p, jnp.sum(p, axis=-1, keepdims=True)

        p1, l1 = masked_probs(s12[:tq])
        p2, l2 = masked_probs(s12[tq:])
        a = p1 - p2 * (lam * l1 * (1.0 / l2))
        o = jnp.dot(a.astype(BF16), v_ref[0:keys, :], preferred_element_type=F32) * (1.0 / l1)
        y = o * lax.rsqrt(jnp.mean(o * o, axis=-1, keepdims=True) + EPS) * g
        o_ref[r0:r1, :] = (y * (1.0 - lam_init)).astype(BF16)

    _software_pipeline(len(q_blocks), lambda i: scores(*q_blocks[i]), lambda i, s12: finish(*q_blocks[i], s12))


def _attention(qk, v, lamp, subln_g, layer, B, L):
    T = B * L
    lam_init = 0.8 - 0.6 * math.exp(-0.3 * layer)
    q_blocks = ((0, CHUNK),) + tuple((r, r + TQ) for r in range(CHUNK, L, TQ))
    kh = A_QK // LANES
    return pl.pallas_call(
        functools.partial(_attn_kernel, lam_init, q_blocks),
        out_shape=jax.ShapeDtypeStruct((T, A_VW), BF16),
        grid=(B, A_HEADS),
        in_specs=[
            pl.BlockSpec((L, LANES), lambda b, h: (b, h)),
            pl.BlockSpec((L, LANES), lambda b, h: (b, kh + h)),
            pl.BlockSpec((L, A_V_DIM), lambda b, h: (b, h)),
            pl.BlockSpec((4, A_HEAD_DIM), lambda b, h: (0, 0)),
            pl.BlockSpec((1, A_V_DIM), lambda b, h: (0, 0)),
        ],
        out_specs=pl.BlockSpec((L, A_V_DIM), lambda b, h: (b, h)),
        compiler_params=_params("arbitrary", "arbitrary"),
        name="diff_attention",
    )(qk, qk, v, lamp, subln_g.reshape(1, A_V_DIM))


def _retention_kernel(nc, q_ref, k_ref, v_ref, rg_ref, gn_ref, decay_ref, tail_ref, qdec_ref,
                      cdec_ref, o_ref):
    decay = decay_ref[...]
    tail = tail_ref[...]
    qdec = qdec_ref[...]
    cdec = cdec_ref[...]
    gn = gn_ref[...]
    nt = (((1,), (1,)), ((), ()))
    tn = (((0,), (0,)), ((), ()))
    state = None
    for c in range(nc):
        rows = slice(c * CHUNK, (c + 1) * CHUNK)
        qc = q_ref[rows, :]
        kc = k_ref[rows, :]
        vc = v_ref[rows, :]
        s = lax.dot_general(qc.astype(BF16), kc.astype(BF16), nt, preferred_element_type=F32) * decay
        o = jnp.dot(s.astype(BF16), vc, preferred_element_type=F32)
        if state is not None:
            o = o + jnp.dot((qc * qdec).astype(BF16), state.astype(BF16), preferred_element_type=F32)
        if c + 1 < nc:
            kv = lax.dot_general((kc * tail).astype(BF16), vc, tn, preferred_element_type=F32)
            state = kv if state is None else cdec * state + kv
        mu = jnp.mean(o, axis=-1, keepdims=True)
        d = o - mu
        var = jnp.mean(d * d, axis=-1, keepdims=True)
        y = d * lax.rsqrt(var + EPS) * gn
        rg = rg_ref[rows, :]
        o_ref[rows, :] = (y * (rg * _sigmoid(rg))).astype(BF16)


def _retention(rqk, rv, gates, gn_g, tabs, B, L):
    T = B * L
    decay, tail, qdec, cdec = tabs
    kh = R_QK // R_QK_DIM
    return pl.pallas_call(
        functools.partial(_retention_kernel, L // CHUNK),
        out_shape=jax.ShapeDtypeStruct((T, R_VW), BF16),
        grid=(B, R_HEADS),
        in_specs=[
            pl.BlockSpec((L, R_QK_DIM), lambda b, h: (b, h)),
            pl.BlockSpec((L, R_QK_DIM), lambda b, h: (b, kh + h)),
            pl.BlockSpec((L, R_V_DIM), lambda b, h: (b, h)),
            pl.BlockSpec((L, R_V_DIM), lambda b, h: (b, h)),
            pl.BlockSpec((1, R_V_DIM), lambda b, h: (0, h)),
            pl.BlockSpec((None, CHUNK, CHUNK), lambda b, h: (h, 0, 0)),
            pl.BlockSpec((None, CHUNK, R_QK_DIM), lambda b, h: (h, 0, 0)),
            pl.BlockSpec((None, CHUNK, R_QK_DIM), lambda b, h: (h, 0, 0)),
            pl.BlockSpec((None, 1, R_V_DIM), lambda b, h: (h, 0, 0)),
        ],
        out_specs=pl.BlockSpec((L, R_V_DIM), lambda b, h: (b, h)),
        compiler_params=_params("arbitrary", "arbitrary"),
        name="retention",
    )(rqk, rqk, rv, gates, gn_g.reshape(1, R_VW), decay, tail, qdec, cdec)


def _merge_kernel(oa_ref, ob_ref, wa_ref, wb_ref, ga_ref, gb_ref, o_ref, was_ref, wbs_ref):
    @pl.when(pl.program_id(1) == 0)
    def _():
        was_ref[...] = wa_ref[...].astype(BF16)
        wbs_ref[...] = wb_ref[...].astype(BF16)

    nr = 2
    rb = o_ref.shape[0] // nr

    def matmuls(r):
        rows = slice(r * rb, (r + 1) * rb)
        return (jnp.dot(oa_ref[rows, :], was_ref[...], preferred_element_type=F32),
                jnp.dot(ob_ref[rows, :], wbs_ref[...], preferred_element_type=F32))

    def gate(r, branches):
        rows = slice(r * rb, (r + 1) * rb)
        ba, bb = branches
        o_ref[rows, :] = (_sigmoid(ga_ref[rows, :]) * ba + _sigmoid(gb_ref[rows, :]) * bb).astype(BF16)

    _software_pipeline(nr, matmuls, gate)


def _merge(oa, ob, w_a, w_b, gates, layer):
    T = oa.shape[0]
    tm, tn = TM_SMALL, TN_FF
    ga0 = (C_GA - C_RG) // tn
    gb0 = (C_GB - C_RG) // tn
    return pl.pallas_call(
        _merge_kernel,
        out_shape=jax.ShapeDtypeStruct((T, D_MODEL), BF16),
        grid=(D_MODEL // tn, T // tm),
        in_specs=[
            pl.BlockSpec((tm, A_VW), lambda n, m: (m, 0)),
            pl.BlockSpec((tm, R_VW), lambda n, m: (m, 0)),
            pl.BlockSpec((None, A_VW, tn), lambda n, m: (layer, 0, n)),
            pl.BlockSpec((None, R_VW, tn), lambda n, m: (layer, 0, n)),
            pl.BlockSpec((tm, tn), lambda n, m: (m, ga0 + n)),
            pl.BlockSpec((tm, tn), lambda n, m: (m, gb0 + n)),
        ],
        out_specs=pl.BlockSpec((tm, tn), lambda n, m: (m, n)),
        scratch_shapes=[pltpu.VMEM((A_VW, tn), BF16), pltpu.VMEM((R_VW, tn), BF16)],
        compiler_params=_params("arbitrary", "arbitrary"),
        name="merge",
    )(oa, ob, w_a, w_b, gates, gates)


def _proj_ln_kernel(n_chunks, a_ref, w_ref, h_ref, g_ref, b_ref, of_ref, ob_ref, ws_ref):
    s = pl.program_id(0)
    ck = w_ref.shape[1]

    @pl.when(s < n_chunks)
    def _():
        col = pl.multiple_of(s * ck, ck)
        ws_ref[:, pl.ds(col, ck)] = w_ref[...].astype(BF16)

    @pl.when(s >= n_chunks)
    def _():
        z = ALPHA * h_ref[...] + jnp.dot(a_ref[...], ws_ref[...], preferred_element_type=F32)
        y = _layer_norm_rows(z, g_ref[...], b_ref[...])
        of_ref[...] = y
        ob_ref[...] = y.astype(BF16)


def _proj_ln(a, kdim, w, layer, hf, g, b, tm, ck, name):
    T = hf.shape[0]
    nck = D_MODEL // ck
    row = lambda s: (jnp.maximum(s - nck, 0), 0)
    return pl.pallas_call(
        functools.partial(_proj_ln_kernel, nck),
        out_shape=(jax.ShapeDtypeStruct((T, D_MODEL), F32), jax.ShapeDtypeStruct((T, D_MODEL), BF16)),
        grid=(nck + T // tm,),
        in_specs=[
            pl.BlockSpec((tm, kdim), row),
            pl.BlockSpec((None, kdim, ck), lambda s: (layer, 0, jnp.minimum(s, nck - 1))),
            pl.BlockSpec((tm, D_MODEL), row),
            pl.BlockSpec((None, 1, D_MODEL), lambda s: (layer, 0, 0)),
            pl.BlockSpec((None, 1, D_MODEL), lambda s: (layer, 0, 0)),
        ],
        out_specs=(pl.BlockSpec((tm, D_MODEL), row), pl.BlockSpec((tm, D_MODEL), row)),
        scratch_shapes=[pltpu.VMEM((kdim, D_MODEL), BF16)],
        compiler_params=_params("arbitrary"),
        name=name,
    )(a, w, hf, g.reshape(DEPTH, 1, D_MODEL), b.reshape(DEPTH, 1, D_MODEL))


def _ffn_up_kernel(rb, h_ref, wg_ref, wv_ref, cwg_ref, cwv_ref, cbg_ref, cbv_ref, o_ref,
                   wgs_ref, wvs_ref, ug_ref, uv_ref):
    m = pl.program_id(1)
    tm = h_ref.shape[0]
    tn = o_ref.shape[1]

    @pl.when(m == 0)
    def _():
        wgs_ref[...] = wg_ref[...].astype(BF16)
        wvs_ref[...] = wv_ref[...].astype(BF16)

    seq_start = (m % 2) == 0

    @pl.when(seq_start)
    def _():
        ug_ref[...] = jnp.zeros_like(ug_ref)
        uv_ref[...] = jnp.zeros_like(uv_ref)

    row = lax.broadcasted_iota(jnp.int32, (rb, tn), 0)
    keep = jnp.logical_or(jnp.logical_not(seq_start), row >= PAD)

    def matmuls(r):
        x = h_ref[r * rb:(r + 1) * rb, :]
        us = [jnp.dot(x, ws_ref[...], preferred_element_type=F32) for ws_ref in (wgs_ref, wvs_ref)]
        if r == 0:
            us = [jnp.where(keep, u, 0.0) for u in us]
        return us

    def conv(u, prev, cw_ref, cb_ref):
        u1 = jnp.where(row == 0, prev[HALO - 1:HALO], pltpu.roll(u, 1, 0))
        u2 = jnp.where(row == 0, prev[HALO - 2:HALO - 1],
                       jnp.where(row == 1, prev[HALO - 1:HALO], pltpu.roll(u, 2, 0)))
        cw = cw_ref[...]
        return u2 * cw[0:1] + u1 * cw[1:2] + u * cw[2:3] + cb_ref[...]

    def epilogue(r, us, prevs):
        gate = conv(us[0], prevs[0], cwg_ref, cbg_ref)
        val = conv(us[1], prevs[1], cwv_ref, cbv_ref)
        gelu = 0.5 * gate * (1.0 + lax.erf(gate * (2.0 ** -0.5)))
        o_ref[r * rb:(r + 1) * rb, :] = (gelu * val).astype(BF16)

    prevs = [ug_ref[...], uv_ref[...]]

    def conv_gate(r, us):
        epilogue(r, us, prevs)
        prevs[:] = [u[rb - HALO:rb, :] for u in us]

    _software_pipeline(tm // rb, matmuls, conv_gate)
    ug_ref[...] = prevs[0]
    uv_ref[...] = prevs[1]


def _ffn_up(hb, w_up, w_val, conv_w, conv_w_val, conv_b, conv_b_val, layer):
    T = hb.shape[0]
    tm, tn = TM_BIG, TN_FF
    assert PAD <= TM_BIG // 4
    return pl.pallas_call(
        functools.partial(_ffn_up_kernel, TM_BIG // 4),
        out_shape=jax.ShapeDtypeStruct((T, D_FF_PAD), BF16),
        grid=(D_FF_PAD // tn, T // tm),
        in_specs=[
            pl.BlockSpec((tm, D_MODEL), lambda n, m: (m, 0)),
            pl.BlockSpec((None, D_MODEL, tn), lambda n, m: (layer, 0, n)),
            pl.BlockSpec((None, D_MODEL, tn), lambda n, m: (layer, 0, n)),
            pl.BlockSpec((None, CONV_W, tn), lambda n, m: (layer, 0, n)),
            pl.BlockSpec((None, CONV_W, tn), lambda n, m: (layer, 0, n)),
            pl.BlockSpec((None, 1, tn), lambda n, m: (layer, 0, n)),
            pl.BlockSpec((None, 1, tn), lambda n, m: (layer, 0, n)),
        ],
        out_specs=pl.BlockSpec((tm, tn), lambda n, m: (m, n)),
        scratch_shapes=[pltpu.VMEM((D_MODEL, tn), BF16), pltpu.VMEM((D_MODEL, tn), BF16),
                        pltpu.VMEM((HALO, tn), F32), pltpu.VMEM((HALO, tn), F32)],
        compiler_params=_params("arbitrary", "arbitrary"),
        name="ffn_up_conv_geglu",
    )(hb, w_up, w_val, conv_w, conv_w_val, conv_b, conv_b_val)


def _rotary_tables(L):
    pos = (jnp.arange(L) - PAD).astype(F32)
    a_freq = 1.0 / (ROPE_THETA ** (jnp.arange(0, A_HEAD_DIM, 2, dtype=F32) / A_HEAD_DIM))
    ang = pos[:, None] * a_freq[None, :]
    ang = jnp.concatenate([ang] * (LANES // (A_HEAD_DIM // 2)), axis=-1)
    lane = jnp.arange(LANES)
    sign_a = jnp.where((lane & (A_HEAD_DIM // 2)) == 0, -1.0, 1.0).astype(F32)
    cos_a, sin_a = jnp.cos(ang), jnp.sin(ang) * sign_a
    scale = (A_HEAD_DIM ** -0.5) * math.log2(math.e)
    cos_a = jnp.stack([cos_a * scale, cos_a])
    sin_a = jnp.stack([sin_a * scale, sin_a])

    r_freq = 1.0 / (ROPE_THETA ** jnp.linspace(0.0, 1.0, R_QK_DIM // 2, dtype=F32))
    ang = pos[:, None] * r_freq[None, :]
    ang = jnp.concatenate([ang, ang], axis=-1)
    sign_r = jnp.where(lane < R_QK_DIM // 2, -1.0, 1.0).astype(F32)
    cos_r = jnp.broadcast_to(jnp.cos(ang), (2, L, LANES))
    sin_r = jnp.broadcast_to(jnp.sin(ang) * sign_r, (2, L, LANES))
    valid = (jnp.arange(L) >= PAD).astype(F32)
    post_k = jnp.broadcast_to(((R_QK_DIM ** -0.5) * valid)[:, None], (L, LANES))
    post_r = jnp.stack([jnp.ones((L, LANES), F32), post_k])
    return (cos_a, sin_a), (cos_r, sin_r, post_r)


def _retention_tables():
    log_gamma = jnp.log(1.0 - 2.0 ** (-5.0 - jnp.arange(R_HEADS, dtype=F32)))
    p = jnp.arange(CHUNK, dtype=F32)
    diff = p[:, None] - p[None, :]
    lg = log_gamma[:, None, None]
    decay = jnp.where(diff >= 0, jnp.exp(lg * jnp.maximum(diff, 0.0)), 0.0)
    tail = jnp.exp(log_gamma[:, None] * (CHUNK - 1 - p))
    qdec = jnp.exp(log_gamma[:, None] * (p + 1.0))
    tail = jnp.broadcast_to(tail[:, :, None], (R_HEADS, CHUNK, R_QK_DIM))
    qdec = jnp.broadcast_to(qdec[:, :, None], (R_HEADS, CHUNK, R_QK_DIM))
    cdec = jnp.broadcast_to(jnp.exp(log_gamma * CHUNK)[:, None, None], (R_HEADS, 1, R_V_DIM))
    return decay, tail, qdec, cdec


def kernel(x, meta_tokens, ln_emb_g, ln_emb_b, w_in, lam_q1, lam_k1, lam_q2, lam_k2, subln_g, ret_gn_g,
           w_branch_a, w_branch_b, w_out, ln1_g, ln1_b, w_up, conv_w, conv_b, w_down, ln2_g, ln2_b):
    B, seq, _ = x.shape
    nb = 1 + seq // CHUNK
    L = nb * CHUNK
    rot_a, rot_r = _rotary_tables(L)
    ret_tabs = _retention_tables()

    padc = D_FF_PAD - D_FF
    w_val = jnp.pad(w_up[:, :, D_FF:], ((0, 0), (0, 0), (0, padc)))
    conv_w_val = jnp.pad(conv_w[:, :, D_FF:], ((0, 0), (0, 0), (0, padc)))
    conv_b3 = conv_b.reshape(DEPTH, 1, 2 * D_FF)
    conv_b_val = jnp.pad(conv_b3[:, :, D_FF:], ((0, 0), (0, 0), (0, padc)))

    hf, hb = _embed_ln(x, meta_tokens, ln_emb_g, ln_emb_b, nb)
    for l in range(DEPTH):
        qk = _proj(hb, w_in, l, C_AQ, 2 * A_QK, "rot_a", BF16, rot_a, lambda n: n // (A_QK // TN_PROJ))
        av = _proj(hb, w_in, l, C_AV, A_VW, "plain", BF16)
        rqk = _proj(hb, w_in, l, C_RQ, 2 * R_QK, "rot_r", F32, rot_r, lambda n: n // (R_QK // TN_PROJ))
        rv = _proj(hb, w_in, l, C_RV, R_VW, "plain", BF16)
        gates = _proj(hb, w_in, l, C_RG, R_VW + 2 * D_MODEL, "plain", F32)

        lamp = jnp.stack([lam_q1[l], lam_k1[l], lam_q2[l], lam_k2[l]])
        oa = _attention(qk, av, lamp, subln_g[l], l, B, L)
        ob = _retention(rqk, rv, gates, ret_gn_g[l], ret_tabs, B, L)
        merged = _merge(oa, ob, w_branch_a, w_branch_b, gates, l)
        hf, hb = _proj_ln(merged, D_MODEL, w_out, l, hf, ln1_g, ln1_b, TM_SMALL, 512, "out_proj_ln")

        gact = _ffn_up(hb, w_up, w_val, conv_w, conv_w_val, conv_b3, conv_b_val, l)
        hf, hb = _proj_ln(gact, D_FF, w_down, l, hf, ln2_g, ln2_b, TM_SMALL // 2, 128, "ffn_down_ln")
    return hf.reshape(B, L, D_MODEL)[:, CHUNK:]
```

```python
import functools
import math

import jax
import jax.numpy as jnp
import numpy as np
from jax import lax
from jax.experimental import pallas as pl
from jax.experimental.pallas import tpu as pltpu

F32 = jnp.float32
BF16 = jnp.bfloat16

D_MODEL = 2048
DEPTH = 2
N_META = 16
CHUNK = 128
PAD = CHUNK - N_META
A_HEAD_DIM = 64
A_V_DIM = 2 * A_HEAD_DIM
A_HEADS = D_MODEL // A_V_DIM
A_QK = A_HEADS * 2 * A_HEAD_DIM
A_VW = A_HEADS * A_V_DIM
R_QK_DIM = 128
R_V_DIM = 2 * R_QK_DIM
R_HEADS = D_MODEL // R_V_DIM
R_QK = R_HEADS * R_QK_DIM
R_VW = R_HEADS * R_V_DIM
D_FF = 128 * ((8 * D_MODEL // 3 + 127) // 128)
CONV_W = 3
ROPE_THETA = 10000.0
ALPHA = (2 * DEPTH) ** 0.25
EPS = 1e-5
NEG = -1e30

LANES = 128
VMEM_LIMIT = 56 * 1024 * 1024

C_AQ = 0
C_AK = C_AQ + A_QK
C_AV = C_AK + A_QK
C_RQ = C_AV + A_VW
C_RK = C_RQ + R_QK
C_RV = C_RK + R_QK
C_RG = C_RV + R_VW
C_GA = C_RG + R_VW
C_GB = C_GA + D_MODEL
W_IN_COLS = C_GB + D_MODEL

TM_BIG = 1088
TM_SMALL = 544
TN_PROJ = 1024
TN_FF = 512
D_FF_PAD = TN_FF * pl.cdiv(D_FF, TN_FF)
TQ = 256
ROW_SPLIT = 4
HALO = 8


def _params(*sem):
    return pltpu.CompilerParams(dimension_semantics=sem, vmem_limit_bytes=VMEM_LIMIT)


def _layer_norm_rows(z, g, b):
    mu = jnp.mean(z, axis=-1, keepdims=True)
    d = z - mu
    var = jnp.mean(d * d, axis=-1, keepdims=True)
    return d * lax.rsqrt(var + EPS) * g + b


def _sigmoid(x):
    return 1.0 / (1.0 + jnp.exp(-x))


def _software_pipeline(n, produce, consume):
    nxt = produce(0)
    for i in range(n):
        cur = nxt
        if i + 1 < n:
            nxt = produce(i + 1)
        consume(i, cur)


def _embed_ln_kernel(x_ref, meta_ref, g_ref, b_ref, hf_ref, hb_ref):
    i = pl.program_id(1)
    g = g_ref[...]
    b = b_ref[...]

    @pl.when(i == 0)
    def _():
        rows = jnp.concatenate([jnp.zeros((PAD, D_MODEL), F32), meta_ref[...]], axis=0)
        y = _layer_norm_rows(rows, g, b)
        hf_ref[...] = y
        hb_ref[...] = y.astype(BF16)

    @pl.when(i > 0)
    def _():
        y = _layer_norm_rows(x_ref[...], g, b)
        hf_ref[...] = y
        hb_ref[...] = y.astype(BF16)


def _embed_ln(x, meta, g, b, nb):
    B = x.shape[0]
    T = B * nb * CHUNK
    return pl.pallas_call(
        _embed_ln_kernel,
        out_shape=(jax.ShapeDtypeStruct((T, D_MODEL), F32), jax.ShapeDtypeStruct((T, D_MODEL), BF16)),
        grid=(B, nb),
        in_specs=[
            pl.BlockSpec((None, CHUNK, D_MODEL), lambda bi, i: (bi, jnp.maximum(i - 1, 0), 0)),
            pl.BlockSpec((N_META, D_MODEL), lambda bi, i: (0, 0)),
            pl.BlockSpec((1, D_MODEL), lambda bi, i: (0, 0)),
            pl.BlockSpec((1, D_MODEL), lambda bi, i: (0, 0)),
        ],
        out_specs=(
            pl.BlockSpec((CHUNK, D_MODEL), lambda bi, i: (bi * nb + i, 0)),
            pl.BlockSpec((CHUNK, D_MODEL), lambda bi, i: (bi * nb + i, 0)),
        ),
        compiler_params=_params("arbitrary", "arbitrary"),
        name="embed_ln",
    )(x, meta, g.reshape(1, D_MODEL), b.reshape(1, D_MODEL))


def _proj_kernel(kind, h_ref, w_ref, *rest):
    if kind == "plain":
        o_ref, wb_ref = rest
    elif kind == "rot_a":
        cos_ref, sin_ref, o_ref, wb_ref = rest
    else:
        cos_ref, sin_ref, post_ref, o_ref, wb_ref = rest

    @pl.when(pl.program_id(1) == 0)
    def _():
        wb_ref[...] = w_ref[...].astype(BF16)

    tm, tn = o_ref.shape
    rb = tm // ROW_SPLIT
    if kind == "rot_a":
        lane = lax.broadcasted_iota(jnp.int32, (rb, LANES), 1)
        low_half = (lane & (A_HEAD_DIM // 2)) == 0

    def matmul(r):
        return jnp.dot(h_ref[r * rb:(r + 1) * rb, :], wb_ref[...], preferred_element_type=F32)

    def epilogue(r, acc):
        rows = slice(r * rb, (r + 1) * rb)
        if kind == "plain":
            o_ref[rows, :] = acc.astype(o_ref.dtype)
            return
        cos = cos_ref[rows, :]
        sin = sin_ref[rows, :]
        for j in range(tn // LANES):
            t = acc[:, j * LANES:(j + 1) * LANES]
            if kind == "rot_a":
                rot = jnp.where(low_half, pltpu.roll(t, LANES - A_HEAD_DIM // 2, 1),
                                pltpu.roll(t, A_HEAD_DIM // 2, 1))
                y = t * cos + rot * sin
            else:
                y = (t * cos + pltpu.roll(t, R_QK_DIM // 2, 1) * sin) * post_ref[rows, :]
            o_ref[rows, j * LANES:(j + 1) * LANES] = y.astype(o_ref.dtype)

    _software_pipeline(ROW_SPLIT, matmul, epilogue)


def _proj(hb, w_in, layer, col0, ncols, kind, out_dtype, tables=(), variant=None):
    T = hb.shape[0]
    tm, tn = TM_BIG, TN_PROJ
    nt, mt = ncols // tn, T // tm
    tile0 = col0 // tn
    in_specs = [
        pl.BlockSpec((tm, D_MODEL), lambda n, m: (m, 0)),
        pl.BlockSpec((None, D_MODEL, tn), lambda n, m: (layer, 0, tile0 + n)),
    ]
    for _ in tables:
        in_specs.append(pl.BlockSpec((None, tm, LANES), lambda n, m: (variant(n), m % 2, 0)))
    return pl.pallas_call(
        functools.partial(_proj_kernel, kind),
        out_shape=jax.ShapeDtypeStruct((T, ncols), out_dtype),
        grid=(nt, mt),
        in_specs=in_specs,
        out_specs=pl.BlockSpec((tm, tn), lambda n, m: (m, n)),
        scratch_shapes=[pltpu.VMEM((D_MODEL, tn), BF16)],
        compiler_params=_params("arbitrary", "arbitrary"),
        name="proj_" + kind,
    )(hb, w_in, *tables)


def _attn_kernel(lam_init, q_blocks, q_ref, k_ref, v_ref, lamp_ref, g_ref, o_ref):
    lp = lamp_ref[...]
    lam = (jnp.exp(jnp.sum(lp[0:1] * lp[1:2], axis=-1, keepdims=True))
           - jnp.exp(jnp.sum(lp[2:3] * lp[3:4], axis=-1, keepdims=True)) + lam_init)
    lane = lax.broadcasted_iota(jnp.int32, (1, LANES), 1)
    map1 = jnp.where(lane < A_HEAD_DIM, 1.0, 0.0).astype(BF16)
    map2 = jnp.where(lane >= A_HEAD_DIM, 1.0, 0.0).astype(BF16)
    g = g_ref[...]
    nt = (((1,), (1,)), ((), ()))

    def scores(r0, r1):
        q = q_ref[r0:r1, :]
        return lax.dot_general(jnp.concatenate([q * map1, q * map2], axis=0), k_ref[0:r1, :], nt,
                               preferred_element_type=F32)

    def weights(r0, r1, s12):
        tq, keys = r1 - r0, r1
        qq = r0 + lax.broadcasted_iota(jnp.int32, (tq, CHUNK), 0)
        kk = lax.broadcasted_iota(jnp.int32, (tq, CHUNK), 1)

        def masked_probs(s):
            parts = []
            for c0 in range(0, keys, CHUNK):
                piece = s[:, c0:c0 + CHUNK]
                cond = None
                if c0 == 0:
                    cond = kk >= PAD
                    if r0 == 0:
                        cond = cond | (kk == qq)
                if c0 + CHUNK > r0:
                    causal = (kk + c0) <= qq
                    cond = causal if cond is None else (cond & causal)
                if cond is not None:
                    piece = jnp.where(cond, piece, NEG)
                parts.append(piece)
            s = parts[0] if len(parts) == 1 else jnp.concatenate(parts, axis=1)
            p = jnp.exp2(s - jnp.max(s, axis=-1, keepdims=True))
            return p, jnp.sum(p, axis=-1, keepdims=True)

        p1, l1 = masked_probs(s12[:tq])
        p2, l2 = masked_probs(s12[tq:])
        return (p1 - p2 * (lam * l1 * (1.0 / l2))).astype(BF16), 1.0 / l1

    def values(r0, r1, a, inv_l1):
        o = jnp.dot(a, v_ref[0:r1, :], preferred_element_type=F32) * inv_l1
        y = o * lax.rsqrt(jnp.mean(o * o, axis=-1, keepdims=True) + EPS) * g
        o_ref[r0:r1, :] = (y * (1.0 - lam_init)).astype(BF16)

    nq = len(q_blocks)
    s_live, a_live = {}, {}
    for i in range(nq + 2):
        if i < nq:
            s_live[i] = scores(*q_blocks[i])
        if i >= 2:
            values(*q_blocks[i - 2], *a_live.pop(i - 2))
        if 1 <= i <= nq:
            a_live[i - 1] = weights(*q_blocks[i - 1], s_live.pop(i - 1))


def _attention(qk, v, lamp, subln_g, layer, B, L):
    T = B * L
    lam_init = 0.8 - 0.6 * math.exp(-0.3 * layer)
    q_blocks = ((0, CHUNK),) + tuple((r, r + TQ) for r in range(CHUNK, L, TQ))
    kh = A_QK // LANES
    return pl.pallas_call(
        functools.partial(_attn_kernel, lam_init, q_blocks),
        out_shape=jax.ShapeDtypeStruct((T, A_VW), BF16),
        grid=(B, A_HEADS),
        in_specs=[
            pl.BlockSpec((L, LANES), lambda b, h: (b, h)),
            pl.BlockSpec((L, LANES), lambda b, h: (b, kh + h)),
            pl.BlockSpec((L, A_V_DIM), lambda b, h: (b, h)),
            pl.BlockSpec((4, A_HEAD_DIM), lambda b, h: (0, 0)),
            pl.BlockSpec((1, A_V_DIM), lambda b, h: (0, 0)),
        ],
        out_specs=pl.BlockSpec((L, A_V_DIM), lambda b, h: (b, h)),
        compiler_params=_params("arbitrary", "arbitrary"),
        name="diff_attention",
    )(qk, qk, v, lamp, subln_g.reshape(1, A_V_DIM))


def _retention_kernel(nc, q_ref, k_ref, v_ref, rg_ref, gn_ref, decay_ref, tail_ref, qdec_ref,
                      cdec_ref, o_ref):
    decay = decay_ref[...]
    tail = tail_ref[...]
    qdec = qdec_ref[...]
    cdec = cdec_ref[...]
    gn = gn_ref[...]
    nt = (((1,), (1,)), ((), ()))
    tn = (((0,), (0,)), ((), ()))
    state = None
    for c in range(nc):
        rows = slice(c * CHUNK, (c + 1) * CHUNK)
        qc = q_ref[rows, :]
        kc = k_ref[rows, :]
        vc = v_ref[rows, :]
        s = lax.dot_general(qc.astype(BF16), kc.astype(BF16), nt, preferred_element_type=F32) * decay
        o = jnp.dot(s.astype(BF16), vc, preferred_element_type=F32)
        if state is not None:
            o = o + jnp.dot((qc * qdec).astype(BF16), state.astype(BF16), preferred_element_type=F32)
        if c + 1 < nc:
            kv = lax.dot_general((kc * tail).astype(BF16), vc, tn, preferred_element_type=F32)
            state = kv if state is None else cdec * state + kv
        mu = jnp.mean(o, axis=-1, keepdims=True)
        d = o - mu
        var = jnp.mean(d * d, axis=-1, keepdims=True)
        y = d * lax.rsqrt(var + EPS) * gn
        rg = rg_ref[rows, :]
        o_ref[rows, :] = (y * (rg * _sigmoid(rg))).astype(BF16)


def _retention(rqk, rv, gates, gn_g, tabs, B, L):
    T = B * L
    decay, tail, qdec, cdec = tabs
    kh = R_QK // R_QK_DIM
    return pl.pallas_call(
        functools.partial(_retention_kernel, L // CHUNK),
        out_shape=jax.ShapeDtypeStruct((T, R_VW), BF16),
        grid=(B, R_HEADS),
        in_specs=[
            pl.BlockSpec((L, R_QK_DIM), lambda b, h: (b, h)),
            pl.BlockSpec((L, R_QK_DIM), lambda b, h: (b, kh + h)),
            pl.BlockSpec((L, R_V_DIM), lambda b, h: (b, h)),
            pl.BlockSpec((L, R_V_DIM), lambda b, h: (b, h)),
            pl.BlockSpec((1, R_V_DIM), lambda b, h: (0, h)),
            pl.BlockSpec((None, CHUNK, CHUNK), lambda b, h: (h, 0, 0)),
            pl.BlockSpec((None, CHUNK, R_QK_DIM), lambda b, h: (h, 0, 0)),
            pl.BlockSpec((None, CHUNK, R_QK_DIM), lambda b, h: (h, 0, 0)),
            pl.BlockSpec((None, 1, R_V_DIM), lambda b, h: (h, 0, 0)),
        ],
        out_specs=pl.BlockSpec((L, R_V_DIM), lambda b, h: (b, h)),
        compiler_params=_params("arbitrary", "arbitrary"),
        name="retention",
    )(rqk, rqk, rv, gates, gn_g.reshape(1, R_VW), decay, tail, qdec, cdec)


def _merge_kernel(n_chunks, oa_ref, ob_ref, wa_ref, wb_ref, ga_ref, gb_ref, o_ref, was_ref, wbs_ref):
    s = pl.program_id(0)
    ck = wa_ref.shape[1]

    @pl.when(s < n_chunks)
    def _():
        col = pl.multiple_of(s * ck, ck)
        was_ref[:, pl.ds(col, ck)] = wa_ref[...].astype(BF16)

    @pl.when(jnp.logical_and(s >= n_chunks, s < 2 * n_chunks))
    def _():
        col = pl.multiple_of((s - n_chunks) * ck, ck)
        wbs_ref[:, pl.ds(col, ck)] = wb_ref[...].astype(BF16)

    @pl.when(s >= 2 * n_chunks)
    def _():
        def matmul(i):
            x_ref, ws_ref = ((oa_ref, was_ref), (ob_ref, wbs_ref))[i]
            return jnp.dot(x_ref[...], ws_ref[...], preferred_element_type=F32)

        gated = []

        def gate(i, branch):
            gated.append(_sigmoid((ga_ref, gb_ref)[i][...]) * branch)

        _software_pipeline(2, matmul, gate)
        o_ref[...] = (gated[0] + gated[1]).astype(BF16)


def _merge(oa, ob, w_a, w_b, gates, layer):
    T = oa.shape[0]
    tm, ck = TM_SMALL // 2, 512
    nck = D_MODEL // ck
    row = lambda s: (jnp.maximum(s - 2 * nck, 0), 0)
    ga0 = (C_GA - C_RG) // D_MODEL
    gb0 = (C_GB - C_RG) // D_MODEL
    return pl.pallas_call(
        functools.partial(_merge_kernel, nck),
        out_shape=jax.ShapeDtypeStruct((T, D_MODEL), BF16),
        grid=(2 * nck + T // tm,),
        in_specs=[
            pl.BlockSpec((tm, A_VW), row),
            pl.BlockSpec((tm, R_VW), row),
            pl.BlockSpec((None, A_VW, ck), lambda s: (layer, 0, jnp.minimum(s, nck - 1))),
            pl.BlockSpec((None, R_VW, ck), lambda s: (layer, 0, jnp.clip(s - nck, 0, nck - 1))),
            pl.BlockSpec((tm, D_MODEL), lambda s: (jnp.maximum(s - 2 * nck, 0), ga0)),
            pl.BlockSpec((tm, D_MODEL), lambda s: (jnp.maximum(s - 2 * nck, 0), gb0)),
        ],
        out_specs=pl.BlockSpec((tm, D_MODEL), row),
        scratch_shapes=[pltpu.VMEM((A_VW, D_MODEL), BF16), pltpu.VMEM((R_VW, D_MODEL), BF16)],
        compiler_params=_params("arbitrary"),
        name="merge",
    )(oa, ob, w_a, w_b, gates, gates)


def _proj_ln_kernel(n_chunks, a_ref, w_ref, h_ref, g_ref, b_ref, of_ref, ob_ref, ws_ref):
    s = pl.program_id(0)
    ck = w_ref.shape[1]

    @pl.when(s < n_chunks)
    def _():
        col = pl.multiple_of(s * ck, ck)
        ws_ref[:, pl.ds(col, ck)] = w_ref[...].astype(BF16)

    @pl.when(s >= n_chunks)
    def _():
        z = ALPHA * h_ref[...] + jnp.dot(a_ref[...], ws_ref[...], preferred_element_type=F32)
        y = _layer_norm_rows(z, g_ref[...], b_ref[...])
        of_ref[...] = y
        ob_ref[...] = y.astype(BF16)


def _proj_ln(a, kdim, w, layer, hf, g, b, tm, ck, name):
    T = hf.shape[0]
    nck = D_MODEL // ck
    row = lambda s: (jnp.maximum(s - nck, 0), 0)
    return pl.pallas_call(
        functools.partial(_proj_ln_kernel, nck),
        out_shape=(jax.ShapeDtypeStruct((T, D_MODEL), F32), jax.ShapeDtypeStruct((T, D_MODEL), BF16)),
        grid=(nck + T // tm,),
        in_specs=[
            pl.BlockSpec((tm, kdim), row),
            pl.BlockSpec((None, kdim, ck), lambda s: (layer, 0, jnp.minimum(s, nck - 1))),
            pl.BlockSpec((tm, D_MODEL), row),
            pl.BlockSpec((None, 1, D_MODEL), lambda s: (layer, 0, 0)),
            pl.BlockSpec((None, 1, D_MODEL), lambda s: (layer, 0, 0)),
        ],
        out_specs=(pl.BlockSpec((tm, D_MODEL), row), pl.BlockSpec((tm, D_MODEL), row)),
        scratch_shapes=[pltpu.VMEM((kdim, D_MODEL), BF16)],
        compiler_params=_params("arbitrary"),
        name=name,
    )(a, w, hf, g.reshape(DEPTH, 1, D_MODEL), b.reshape(DEPTH, 1, D_MODEL))


def _ffn_up_kernel(h_ref, wg_ref, wv_ref, cwg_ref, cwv_ref, cbg_ref, cbv_ref, o_ref,
                   wgs_ref, wvs_ref, ug_ref, uv_ref):
    m = pl.program_id(1)
    tm = h_ref.shape[0]
    tn = o_ref.shape[1]

    @pl.when(m == 0)
    def _():
        wgs_ref[...] = wg_ref[...].astype(BF16)
        wvs_ref[...] = wv_ref[...].astype(BF16)

    seq_start = (m % 2) == 0

    @pl.when(seq_start)
    def _():
        ug_ref[...] = jnp.zeros_like(ug_ref)
        uv_ref[...] = jnp.zeros_like(uv_ref)

    row8 = lax.broadcasted_iota(jnp.int32, (HALO, tn), 0)
    halves = ((wgs_ref, cwg_ref, cbg_ref, ug_ref), (wvs_ref, cwv_ref, cbv_ref, uv_ref))
    gates = []

    def matmul(half):
        u = jnp.dot(h_ref[...], halves[half][0][...], preferred_element_type=F32)
        return jnp.concatenate([jnp.where(seq_start, 0.0, u[:PAD]), u[PAD:]], axis=0)

    def shifted(u, prev, k):
        rolled = pltpu.roll(u, k, 0)
        head = rolled[:HALO]
        for t in range(k):
            head = jnp.where(row8 == t, prev[HALO - k + t:HALO - k + t + 1], head)
        return jnp.concatenate([head, rolled[HALO:]], axis=0)

    def conv_geglu(half, u):
        _, cw_ref, cb_ref, carry_ref = halves[half]
        prev = carry_ref[...]
        carry_ref[...] = u[tm - HALO:tm, :]
        scale = 1.0 if half == 0 else 0.5
        cw = cw_ref[...] * scale
        c = shifted(u, prev, 2) * cw[0:1] + shifted(u, prev, 1) * cw[1:2] + u * cw[2:3] + cb_ref[...] * scale
        if half == 0:
            gates.append(c)
        else:
            gate = gates.pop()
            o_ref[...] = (gate * (1.0 + lax.erf(gate * (2.0 ** -0.5))) * c).astype(BF16)

    _software_pipeline(2, matmul, conv_geglu)


def _ffn_up(hb, w_up, w_val, conv_w, conv_w_val, conv_b, conv_b_val, layer):
    T = hb.shape[0]
    tm, tn = TM_BIG, TN_FF
    return pl.pallas_call(
        _ffn_up_kernel,
        out_shape=jax.ShapeDtypeStruct((T, D_FF_PAD), BF16),
        grid=(D_FF_PAD // tn, T // tm),
        in_specs=[
            pl.BlockSpec((tm, D_MODEL), lambda n, m: (m, 0)),
            pl.BlockSpec((None, D_MODEL, tn), lambda n, m: (layer, 0, n)),
            pl.BlockSpec((None, D_MODEL, tn), lambda n, m: (layer, 0, n)),
            pl.BlockSpec((None, CONV_W, tn), lambda n, m: (layer, 0, n)),
            pl.BlockSpec((None, CONV_W, tn), lambda n, m: (layer, 0, n)),
            pl.BlockSpec((None, 1, tn), lambda n, m: (layer, 0, n)),
            pl.BlockSpec((None, 1, tn), lambda n, m: (layer, 0, n)),
        ],
        out_specs=pl.BlockSpec((tm, tn), lambda n, m: (m, n)),
        scratch_shapes=[pltpu.VMEM((D_MODEL, tn), BF16), pltpu.VMEM((D_MODEL, tn), BF16),
                        pltpu.VMEM((HALO, tn), F32), pltpu.VMEM((HALO, tn), F32)],
        compiler_params=_params("arbitrary", "arbitrary"),
        name="ffn_up_conv_geglu",
    )(hb, w_up, w_val, conv_w, conv_w_val, conv_b, conv_b_val)


def _const(a):
    return jnp.asarray(np.ascontiguousarray(a, dtype=np.float32))


def _rotary_tables(L):
    pos = (np.arange(L) - PAD).astype(np.float64)
    a_freq = 1.0 / (ROPE_THETA ** (np.arange(0, A_HEAD_DIM, 2, dtype=np.float64) / A_HEAD_DIM))
    ang = pos[:, None] * a_freq[None, :]
    ang = np.concatenate([ang] * (LANES // (A_HEAD_DIM // 2)), axis=-1)
    lane = np.arange(LANES)
    sign_a = np.where((lane & (A_HEAD_DIM // 2)) == 0, -1.0, 1.0)
    cos_a, sin_a = np.cos(ang), np.sin(ang) * sign_a
    scale = (A_HEAD_DIM ** -0.5) * math.log2(math.e)
    cos_a = np.stack([cos_a * scale, cos_a])
    sin_a = np.stack([sin_a * scale, sin_a])

    r_freq = 1.0 / (ROPE_THETA ** np.linspace(0.0, 1.0, R_QK_DIM // 2, dtype=np.float64))
    ang = pos[:, None] * r_freq[None, :]
    ang = np.concatenate([ang, ang], axis=-1)
    sign_r = np.where(lane < R_QK_DIM // 2, -1.0, 1.0)
    cos_r = np.broadcast_to(np.cos(ang), (2, L, LANES))
    sin_r = np.broadcast_to(np.sin(ang) * sign_r, (2, L, LANES))
    valid = (np.arange(L) >= PAD).astype(np.float64)
    post_k = np.broadcast_to(((R_QK_DIM ** -0.5) * valid)[:, None], (L, LANES))
    post_r = np.stack([np.ones((L, LANES)), post_k])
    return (_const(cos_a), _const(sin_a)), (_const(cos_r), _const(sin_r), _const(post_r))


def _retention_tables():
    log_gamma = np.log(1.0 - 2.0 ** (-5.0 - np.arange(R_HEADS, dtype=np.float64)))
    p = np.arange(CHUNK, dtype=np.float64)
    diff = p[:, None] - p[None, :]
    lg = log_gamma[:, None, None]
    decay = np.where(diff >= 0, np.exp(lg * np.maximum(diff, 0.0)), 0.0)
    tail = np.exp(log_gamma[:, None] * (CHUNK - 1 - p))
    qdec = np.exp(log_gamma[:, None] * (p + 1.0))
    tail = np.broadcast_to(tail[:, :, None], (R_HEADS, CHUNK, R_QK_DIM))
    qdec = np.broadcast_to(qdec[:, :, None], (R_HEADS, CHUNK, R_QK_DIM))
    cdec = np.broadcast_to(np.exp(log_gamma * CHUNK)[:, None, None], (R_HEADS, 1, R_V_DIM))
    return _const(decay), _const(tail), _const(qdec), _const(cdec)


def kernel(x, meta_tokens, ln_emb_g, ln_emb_b, w_in, lam_q1, lam_k1, lam_q2, lam_k2, subln_g, ret_gn_g,
           w_branch_a, w_branch_b, w_out, ln1_g, ln1_b, w_up, conv_w, conv_b, w_down, ln2_g, ln2_b):
    B, seq, _ = x.shape
    nb = 1 + seq // CHUNK
    L = nb * CHUNK
    rot_a, rot_r = _rotary_tables(L)
    ret_tabs = _retention_tables()

    padc = D_FF_PAD - D_FF
    w_val = jnp.pad(w_up[:, :, D_FF:], ((0, 0), (0, 0), (0, padc)))
    conv_w_val = jnp.pad(conv_w[:, :, D_FF:], ((0, 0), (0, 0), (0, padc)))
    conv_b3 = conv_b.reshape(DEPTH, 1, 2 * D_FF)
    conv_b_val = jnp.pad(conv_b3[:, :, D_FF:], ((0, 0), (0, 0), (0, padc)))

    hf, hb = _embed_ln(x, meta_tokens, ln_emb_g, ln_emb_b, nb)
    for l in range(DEPTH):
        qk = _proj(hb, w_in, l, C_AQ, 2 * A_QK, "rot_a", BF16, rot_a, lambda n: n // (A_QK // TN_PROJ))
        av = _proj(hb, w_in, l, C_AV, A_VW, "plain", BF16)
        rqk = _proj(hb, w_in, l, C_RQ, 2 * R_QK, "rot_r", F32, rot_r, lambda n: n // (R_QK // TN_PROJ))
        rv = _proj(hb, w_in, l, C_RV, R_VW, "plain", BF16)
        gates = _proj(hb, w_in, l, C_RG, R_VW + 2 * D_MODEL, "plain", F32)

        lamp = jnp.stack([lam_q1[l], lam_k1[l], lam_q2[l], lam_k2[l]])
        oa = _attention(qk, av, lamp, subln_g[l], l, B, L)
        ob = _retention(rqk, rv, gates, ret_gn_g[l], ret_tabs, B, L)
        merged = _merge(oa, ob, w_branch_a, w_branch_b, gates, l)
        hf, hb = _proj_ln(merged, D_MODEL, w_out, l, hf, ln1_g, ln1_b, TM_SMALL, 512, "out_proj_ln")

        gact = _ffn_up(hb, w_up, w_val, conv_w, conv_w_val, conv_b3, conv_b_val, l)
        hf, hb = _proj_ln(gact, D_FF, w_down, l, hf, ln2_g, ln2_b, TM_SMALL // 2, 128, "ffn_down_ln")
    return hf.reshape(B, L, D_MODEL)[:, CHUNK:]
```

```python
import functools
import math

import jax
import jax.numpy as jnp
import numpy as np
from jax import lax
from jax.experimental import pallas as pl
from jax.experimental.pallas import tpu as pltpu

F32 = jnp.float32
BF16 = jnp.bfloat16

D_MODEL = 2048
DEPTH = 2
N_META = 16
CHUNK = 128
PAD = CHUNK - N_META
A_HEAD_DIM = 64
A_V_DIM = 2 * A_HEAD_DIM
A_HEADS = D_MODEL // A_V_DIM
A_QK = A_HEADS * 2 * A_HEAD_DIM
A_VW = A_HEADS * A_V_DIM
R_QK_DIM = 128
R_V_DIM = 2 * R_QK_DIM
R_HEADS = D_MODEL // R_V_DIM
R_QK = R_HEADS * R_QK_DIM
R_VW = R_HEADS * R_V_DIM
D_FF = 128 * ((8 * D_MODEL // 3 + 127) // 128)
CONV_W = 3
ROPE_THETA = 10000.0
ALPHA = (2 * DEPTH) ** 0.25
EPS = 1e-5
NEG = -1e30

LANES = 128
VMEM_LIMIT = 56 * 1024 * 1024

C_AQ = 0
C_AK = C_AQ + A_QK
C_AV = C_AK + A_QK
C_RQ = C_AV + A_VW
C_RK = C_RQ + R_QK
C_RV = C_RK + R_QK
C_RG = C_RV + R_VW
C_GA = C_RG + R_VW
C_GB = C_GA + D_MODEL
W_IN_COLS = C_GB + D_MODEL

TM_BIG = 1088
TM_SMALL = 544
TN_PROJ = 1024
TN_FF = 512
TQ = 256
ROW_SPLIT = 4
HALO = 8


def _params(*sem):
    return pltpu.CompilerParams(dimension_semantics=sem, vmem_limit_bytes=VMEM_LIMIT)


def _layer_norm_rows(z, g, b):
    mu = jnp.mean(z, axis=-1, keepdims=True)
    d = z - mu
    var = jnp.mean(d * d, axis=-1, keepdims=True)
    return d * lax.rsqrt(var + EPS) * g + b


def _sigmoid(x):
    return 1.0 / (1.0 + jnp.exp(-x))


def _software_pipeline(n, produce, consume):
    nxt = produce(0)
    for i in range(n):
        cur = nxt
        if i + 1 < n:
            nxt = produce(i + 1)
        consume(i, cur)


def _embed_ln_kernel(x_ref, meta_ref, g_ref, b_ref, hf_ref, hb_ref):
    i = pl.program_id(1)
    g = g_ref[...]
    b = b_ref[...]

    @pl.when(i == 0)
    def _():
        rows = jnp.concatenate([jnp.zeros((PAD, D_MODEL), F32), meta_ref[...]], axis=0)
        y = _layer_norm_rows(rows, g, b)
        hf_ref[...] = y
        hb_ref[...] = y.astype(BF16)

    @pl.when(i > 0)
    def _():
        y = _layer_norm_rows(x_ref[...], g, b)
        hf_ref[...] = y
        hb_ref[...] = y.astype(BF16)


def _embed_ln(x, meta, g, b, nb):
    B = x.shape[0]
    T = B * nb * CHUNK
    return pl.pallas_call(
        _embed_ln_kernel,
        out_shape=(jax.ShapeDtypeStruct((T, D_MODEL), F32), jax.ShapeDtypeStruct((T, D_MODEL), BF16)),
        grid=(B, nb),
        in_specs=[
            pl.BlockSpec((None, CHUNK, D_MODEL), lambda bi, i: (bi, jnp.maximum(i - 1, 0), 0)),
            pl.BlockSpec((N_META, D_MODEL), lambda bi, i: (0, 0)),
            pl.BlockSpec((1, D_MODEL), lambda bi, i: (0, 0)),
            pl.BlockSpec((1, D_MODEL), lambda bi, i: (0, 0)),
        ],
        out_specs=(
            pl.BlockSpec((CHUNK, D_MODEL), lambda bi, i: (bi * nb + i, 0)),
            pl.BlockSpec((CHUNK, D_MODEL), lambda bi, i: (bi * nb + i, 0)),
        ),
        compiler_params=_params("arbitrary", "arbitrary"),
        name="embed_ln",
    )(x, meta, g.reshape(1, D_MODEL), b.reshape(1, D_MODEL))


def _proj_kernel(kind, h_ref, w_ref, *rest):
    if kind == "plain":
        o_ref, wb_ref = rest
    elif kind == "rot_a":
        cos_ref, sin_ref, o_ref, wb_ref = rest
    else:
        cos_ref, sin_ref, post_ref, o_ref, wb_ref = rest

    @pl.when(pl.program_id(1) == 0)
    def _():
        wb_ref[...] = w_ref[...].astype(BF16)

    tm, tn = o_ref.shape
    rb = tm // ROW_SPLIT
    if kind == "rot_a":
        lane = lax.broadcasted_iota(jnp.int32, (rb, LANES), 1)
        low_half = (lane & (A_HEAD_DIM // 2)) == 0

    def matmul(r):
        return jnp.dot(h_ref[r * rb:(r + 1) * rb, :], wb_ref[...], preferred_element_type=F32)

    def epilogue(r, acc):
        rows = slice(r * rb, (r + 1) * rb)
        if kind == "plain":
            o_ref[rows, :] = acc.astype(o_ref.dtype)
            return
        cos = cos_ref[rows, :]
        sin = sin_ref[rows, :]
        for j in range(tn // LANES):
            t = acc[:, j * LANES:(j + 1) * LANES]
            if kind == "rot_a":
                rot = jnp.where(low_half, pltpu.roll(t, LANES - A_HEAD_DIM // 2, 1),
                                pltpu.roll(t, A_HEAD_DIM // 2, 1))
                y = t * cos + rot * sin
            else:
                y = (t * cos + pltpu.roll(t, R_QK_DIM // 2, 1) * sin) * post_ref[rows, :]
            o_ref[rows, j * LANES:(j + 1) * LANES] = y.astype(o_ref.dtype)

    _software_pipeline(ROW_SPLIT, matmul, epilogue)


def _proj(hb, w_in, layer, col0, ncols, kind, out_dtype, tables=(), variant=None):
    T = hb.shape[0]
    tm, tn = TM_BIG, TN_PROJ
    nt, mt = ncols // tn, T // tm
    tile0 = col0 // tn
    in_specs = [
        pl.BlockSpec((tm, D_MODEL), lambda n, m: (m, 0)),
        pl.BlockSpec((None, D_MODEL, tn), lambda n, m: (layer, 0, tile0 + n)),
    ]
    for _ in tables:
        in_specs.append(pl.BlockSpec((None, tm, LANES), lambda n, m: (variant(n), m % 2, 0)))
    return pl.pallas_call(
        functools.partial(_proj_kernel, kind),
        out_shape=jax.ShapeDtypeStruct((T, ncols), out_dtype),
        grid=(nt, mt),
        in_specs=in_specs,
        out_specs=pl.BlockSpec((tm, tn), lambda n, m: (m, n)),
        scratch_shapes=[pltpu.VMEM((D_MODEL, tn), BF16)],
        compiler_params=_params("arbitrary", "arbitrary"),
        name="proj_" + kind,
    )(hb, w_in, *tables)


def _attn_kernel(lam_init, q_blocks, q_ref, k_ref, v_ref, lamp_ref, g_ref, o_ref):
    lp = lamp_ref[...]
    lam = (jnp.exp(jnp.sum(lp[0:1] * lp[1:2], axis=-1, keepdims=True))
           - jnp.exp(jnp.sum(lp[2:3] * lp[3:4], axis=-1, keepdims=True)) + lam_init)
    lane = lax.broadcasted_iota(jnp.int32, (1, LANES), 1)
    map1 = jnp.where(lane < A_HEAD_DIM, 1.0, 0.0).astype(BF16)
    map2 = jnp.where(lane >= A_HEAD_DIM, 1.0, 0.0).astype(BF16)
    g = g_ref[...]
    nt = (((1,), (1,)), ((), ()))

    def scores(r0, r1):
        q = q_ref[r0:r1, :]
        return lax.dot_general(jnp.concatenate([q * map1, q * map2], axis=0), k_ref[0:r1, :], nt,
                               preferred_element_type=F32)

    def weights(r0, r1, s12):
        tq, keys = r1 - r0, r1
        qq = r0 + lax.broadcasted_iota(jnp.int32, (tq, CHUNK), 0)
        kk = lax.broadcasted_iota(jnp.int32, (tq, CHUNK), 1)

        def masked_probs(s):
            parts = []
            for c0 in range(0, keys, CHUNK):
                piece = s[:, c0:c0 + CHUNK]
                cond = None
                if c0 == 0:
                    cond = kk >= PAD
                    if r0 == 0:
                        cond = cond | (kk == qq)
                if c0 + CHUNK > r0:
                    causal = (kk + c0) <= qq
                    cond = causal if cond is None else (cond & causal)
                if cond is not None:
                    piece = jnp.where(cond, piece, NEG)
                parts.append(piece)
            s = parts[0] if len(parts) == 1 else jnp.concatenate(parts, axis=1)
            p = jnp.exp2(s - jnp.max(s, axis=-1, keepdims=True))
            return p, jnp.sum(p, axis=-1, keepdims=True)

        p1, l1 = masked_probs(s12[:tq])
        p2, l2 = masked_probs(s12[tq:])
        return (p1 - p2 * (lam * l1 * (1.0 / l2))).astype(BF16), 1.0 / l1

    def values(r0, r1, a, inv_l1):
        o = jnp.dot(a, v_ref[0:r1, :], preferred_element_type=F32) * inv_l1
        y = o * lax.rsqrt(jnp.mean(o * o, axis=-1, keepdims=True) + EPS) * g
        o_ref[r0:r1, :] = (y * (1.0 - lam_init)).astype(BF16)

    nq = len(q_blocks)
    s_live, a_live = {}, {}
    for i in range(nq + 2):
        if i < nq:
            s_live[i] = scores(*q_blocks[i])
        if i >= 2:
            values(*q_blocks[i - 2], *a_live.pop(i - 2))
        if 1 <= i <= nq:
            a_live[i - 1] = weights(*q_blocks[i - 1], s_live.pop(i - 1))


def _attention(qk, v, lamp, subln_g, layer, B, L):
    T = B * L
    lam_init = 0.8 - 0.6 * math.exp(-0.3 * layer)
    q_blocks = ((0, CHUNK),) + tuple((r, r + TQ) for r in range(CHUNK, L, TQ))
    kh = A_QK // LANES
    return pl.pallas_call(
        functools.partial(_attn_kernel, lam_init, q_blocks),
        out_shape=jax.ShapeDtypeStruct((T, A_VW), BF16),
        grid=(B, A_HEADS),
        in_specs=[
            pl.BlockSpec((L, LANES), lambda b, h: (b, h)),
            pl.BlockSpec((L, LANES), lambda b, h: (b, kh + h)),
            pl.BlockSpec((L, A_V_DIM), lambda b, h: (b, h)),
            pl.BlockSpec((4, A_HEAD_DIM), lambda b, h: (0, 0)),
            pl.BlockSpec((1, A_V_DIM), lambda b, h: (0, 0)),
        ],
        out_specs=pl.BlockSpec((L, A_V_DIM), lambda b, h: (b, h)),
        compiler_params=_params("arbitrary", "arbitrary"),
        name="diff_attention",
    )(qk, qk, v, lamp, subln_g.reshape(1, A_V_DIM))


def _retention_kernel(nc, q_ref, k_ref, v_ref, rg_ref, gn_ref, decay_ref, tail_ref, qdec_ref,
                      cdec_ref, o_ref):
    decay = decay_ref[...]
    tail = tail_ref[...]
    qdec = qdec_ref[...]
    cdec = cdec_ref[...]
    gn = gn_ref[...]
    nt = (((1,), (1,)), ((), ()))
    tn = (((0,), (0,)), ((), ()))
    state = None
    for c in range(nc):
        rows = slice(c * CHUNK, (c + 1) * CHUNK)
        qc = q_ref[rows, :]
        kc = k_ref[rows, :]
        vc = v_ref[rows, :]
        s = lax.dot_general(qc.astype(BF16), kc.astype(BF16), nt, preferred_element_type=F32) * decay
        o = jnp.dot(s.astype(BF16), vc, preferred_element_type=F32)
        if state is not None:
            o = o + jnp.dot((qc * qdec).astype(BF16), state.astype(BF16), preferred_element_type=F32)
        if c + 1 < nc:
            kv = lax.dot_general((kc * tail).astype(BF16), vc, tn, preferred_element_type=F32)
            state = kv if state is None else cdec * state + kv
        mu = jnp.mean(o, axis=-1, keepdims=True)
        d = o - mu
        var = jnp.mean(d * d, axis=-1, keepdims=True)
        y = d * lax.rsqrt(var + EPS) * gn
        rg = rg_ref[rows, :]
        o_ref[rows, :] = (y * (rg * _sigmoid(rg))).astype(BF16)


def _retention(rqk, rv, gates, gn_g, tabs, B, L):
    T = B * L
    decay, tail, qdec, cdec = tabs
    kh = R_QK // R_QK_DIM
    return pl.pallas_call(
        functools.partial(_retention_kernel, L // CHUNK),
        out_shape=jax.ShapeDtypeStruct((T, R_VW), BF16),
        grid=(B, R_HEADS),
        in_specs=[
            pl.BlockSpec((L, R_QK_DIM), lambda b, h: (b, h)),
            pl.BlockSpec((L, R_QK_DIM), lambda b, h: (b, kh + h)),
            pl.BlockSpec((L, R_V_DIM), lambda b, h: (b, h)),
            pl.BlockSpec((L, R_V_DIM), lambda b, h: (b, h)),
            pl.BlockSpec((1, R_V_DIM), lambda b, h: (0, h)),
            pl.BlockSpec((None, CHUNK, CHUNK), lambda b, h: (h, 0, 0)),
            pl.BlockSpec((None, CHUNK, R_QK_DIM), lambda b, h: (h, 0, 0)),
            pl.BlockSpec((None, CHUNK, R_QK_DIM), lambda b, h: (h, 0, 0)),
            pl.BlockSpec((None, 1, R_V_DIM), lambda b, h: (h, 0, 0)),
        ],
        out_specs=pl.BlockSpec((L, R_V_DIM), lambda b, h: (b, h)),
        compiler_params=_params("arbitrary", "arbitrary"),
        name="retention",
    )(rqk, rqk, rv, gates, gn_g.reshape(1, R_VW), decay, tail, qdec, cdec)


def _merge_kernel(n_chunks, oa_ref, ob_ref, wa_ref, wb_ref, ga_ref, gb_ref, o_ref, was_ref, wbs_ref):
    s = pl.program_id(0)
    ck = wa_ref.shape[1]

    @pl.when(s < n_chunks)
    def _():
        col = pl.multiple_of(s * ck, ck)
        was_ref[:, pl.ds(col, ck)] = wa_ref[...].astype(BF16)

    @pl.when(jnp.logical_and(s >= n_chunks, s < 2 * n_chunks))
    def _():
        col = pl.multiple_of((s - n_chunks) * ck, ck)
        wbs_ref[:, pl.ds(col, ck)] = wb_ref[...].astype(BF16)

    @pl.when(s >= 2 * n_chunks)
    def _():
        def matmul(i):
            x_ref, ws_ref = ((oa_ref, was_ref), (ob_ref, wbs_ref))[i]
            return jnp.dot(x_ref[...], ws_ref[...], preferred_element_type=F32)

        gated = []

        def gate(i, branch):
            gated.append(_sigmoid((ga_ref, gb_ref)[i][...]) * branch)

        _software_pipeline(2, matmul, gate)
        o_ref[...] = (gated[0] + gated[1]).astype(BF16)


def _merge(oa, ob, w_a, w_b, gates, layer):
    T = oa.shape[0]
    tm, ck = TM_SMALL // 2, 512
    nck = D_MODEL // ck
    row = lambda s: (jnp.maximum(s - 2 * nck, 0), 0)
    ga0 = (C_GA - C_RG) // D_MODEL
    gb0 = (C_GB - C_RG) // D_MODEL
    return pl.pallas_call(
        functools.partial(_merge_kernel, nck),
        out_shape=jax.ShapeDtypeStruct((T, D_MODEL), BF16),
        grid=(2 * nck + T // tm,),
        in_specs=[
            pl.BlockSpec((tm, A_VW), row),
            pl.BlockSpec((tm, R_VW), row),
            pl.BlockSpec((None, A_VW, ck), lambda s: (layer, 0, jnp.minimum(s, nck - 1))),
            pl.BlockSpec((None, R_VW, ck), lambda s: (layer, 0, jnp.clip(s - nck, 0, nck - 1))),
            pl.BlockSpec((tm, D_MODEL), lambda s: (jnp.maximum(s - 2 * nck, 0), ga0)),
            pl.BlockSpec((tm, D_MODEL), lambda s: (jnp.maximum(s - 2 * nck, 0), gb0)),
        ],
        out_specs=pl.BlockSpec((tm, D_MODEL), row),
        scratch_shapes=[pltpu.VMEM((A_VW, D_MODEL), BF16), pltpu.VMEM((R_VW, D_MODEL), BF16)],
        compiler_params=_params("arbitrary"),
        name="merge",
    )(oa, ob, w_a, w_b, gates, gates)


def _proj_ln_kernel(n_chunks, a_ref, w_ref, h_ref, g_ref, b_ref, of_ref, *rest):
    ob_ref, ws_ref = rest if len(rest) == 2 else (None, rest[0])
    s = pl.program_id(0)
    ck = w_ref.shape[1]

    @pl.when(s < n_chunks)
    def _():
        col = pl.multiple_of(s * ck, ck)
        ws_ref[:, pl.ds(col, ck)] = w_ref[...].astype(BF16)

    @pl.when(s >= n_chunks)
    def _():
        z = ALPHA * h_ref[...] + jnp.dot(a_ref[...], ws_ref[...], preferred_element_type=F32)
        y = _layer_norm_rows(z, g_ref[...], b_ref[...])
        of_ref[...] = y
        if ob_ref is not None:
            ob_ref[...] = y.astype(BF16)


def _proj_ln(a, kdim, w, layer, hf, g, b, tm, ck, name):
    T = hf.shape[0]
    nck = D_MODEL // ck
    row = lambda s: (jnp.maximum(s - nck, 0), 0)
    return pl.pallas_call(
        functools.partial(_proj_ln_kernel, nck),
        out_shape=(jax.ShapeDtypeStruct((T, D_MODEL), F32), jax.ShapeDtypeStruct((T, D_MODEL), BF16)),
        grid=(nck + T // tm,),
        in_specs=[
            pl.BlockSpec((tm, kdim), row),
            pl.BlockSpec((None, kdim, ck), lambda s: (layer, 0, jnp.minimum(s, nck - 1))),
            pl.BlockSpec((tm, D_MODEL), row),
            pl.BlockSpec((None, 1, D_MODEL), lambda s: (layer, 0, 0)),
            pl.BlockSpec((None, 1, D_MODEL), lambda s: (layer, 0, 0)),
        ],
        out_specs=(pl.BlockSpec((tm, D_MODEL), row), pl.BlockSpec((tm, D_MODEL), row)),
        scratch_shapes=[pltpu.VMEM((kdim, D_MODEL), BF16)],
        compiler_params=_params("arbitrary"),
        name=name,
    )(a, w, hf, g.reshape(DEPTH, 1, D_MODEL), b.reshape(DEPTH, 1, D_MODEL))


def _proj_ln_final(a, kdim, w, layer, hf, g, b, tm, ck, nb, name):
    T = hf.shape[0]
    B = T // (nb * CHUNK)
    seq = (nb - 1) * CHUNK
    nck = D_MODEL // ck
    per_seq = seq // tm

    def rows(cols):
        def index(s):
            t = jnp.maximum(s - nck, 0)
            return ((t // per_seq) * nb + 1 + (t % per_seq) * (tm // CHUNK)) * CHUNK, 0
        return pl.BlockSpec((pl.Element(tm), pl.Element(cols)), index)

    return pl.pallas_call(
        functools.partial(_proj_ln_kernel, nck),
        out_shape=jax.ShapeDtypeStruct((B * seq, D_MODEL), F32),
        grid=(nck + B * per_seq,),
        in_specs=[
            rows(kdim),
            pl.BlockSpec((None, kdim, ck), lambda s: (layer, 0, jnp.minimum(s, nck - 1))),
            rows(D_MODEL),
            pl.BlockSpec((None, 1, D_MODEL), lambda s: (layer, 0, 0)),
            pl.BlockSpec((None, 1, D_MODEL), lambda s: (layer, 0, 0)),
        ],
        out_specs=pl.BlockSpec((tm, D_MODEL), lambda s: (jnp.maximum(s - nck, 0), 0)),
        scratch_shapes=[pltpu.VMEM((kdim, D_MODEL), BF16)],
        compiler_params=_params("arbitrary"),
        name=name,
    )(a, w, hf, g.reshape(DEPTH, 1, D_MODEL), b.reshape(DEPTH, 1, D_MODEL))


def _ffn_up_kernel(rb, overlap, h_ref, wg_ref, wv_ref, cwg_ref, cwv_ref, cbg_ref, cbv_ref, o_ref,
                   wgs_ref, wvs_ref):
    tn = o_ref.shape[1]

    @pl.when(pl.program_id(1) == 0)
    def _():
        wgs_ref[...] = wg_ref[...].astype(BF16)
        wvs_ref[...] = wv_ref[...].astype(BF16)

    row8 = lax.broadcasted_iota(jnp.int32, (HALO, tn), 0)
    halves = ((wgs_ref, cwg_ref, cbg_ref), (wvs_ref, cwv_ref, cbv_ref))
    prevs = [jnp.zeros((HALO, tn), F32)] * 2
    gates = []

    def matmul(i):
        r, half = divmod(i, 2)
        u = jnp.dot(h_ref[r * rb:(r + 1) * rb, :], halves[half][0][...], preferred_element_type=F32)
        if r == 0:
            u = jnp.concatenate([jnp.zeros((PAD, tn), F32), u[PAD:]], axis=0)
        return u

    def shifted(u, prev, k):
        rolled = pltpu.roll(u, k, 0)
        head = rolled[:HALO]
        for t in range(k):
            head = jnp.where(row8 == t, prev[HALO - k + t:HALO - k + t + 1], head)
        return jnp.concatenate([head, rolled[HALO:]], axis=0)

    def conv_geglu(i, u):
        r, half = divmod(i, 2)
        _, cw_ref, cb_ref = halves[half]
        prev = prevs[half]
        prevs[half] = u[rb - HALO:rb, :]
        scale = 1.0 if half == 0 else 0.5
        cw = cw_ref[...] * scale
        c = shifted(u, prev, 2) * cw[0:1] + shifted(u, prev, 1) * cw[1:2] + u * cw[2:3] + cb_ref[...] * scale
        if half == 0:
            gates.append(c)
        else:
            gate = gates.pop()
            y = gate * (1.0 + lax.erf(gate * (2.0 ** -0.5))) * c
            moved = jnp.concatenate([y[:, overlap:], jnp.zeros((rb, overlap), F32)], axis=1)
            y = jnp.where(pl.program_id(0) == pl.num_programs(0) - 1, moved, y)
            o_ref[r * rb:(r + 1) * rb, :] = y.astype(BF16)

    _software_pipeline(2 * (h_ref.shape[0] // rb), matmul, conv_geglu)


def _ffn_up(hb, w_up, conv_w, conv_b, layer, L):
    T = hb.shape[0]
    tm, tn = L, TN_FF
    assert PAD <= TM_BIG and L % TM_BIG == 0
    nt = pl.cdiv(D_FF, tn)
    overlap = nt * tn - D_FF
    lane_tile = lambda n: jnp.minimum(n * (tn // LANES), (D_FF - tn) // LANES)

    def cols(rows, half):
        return pl.BlockSpec((None, pl.Element(rows), pl.Element(tn)),
                            lambda n, m: (layer, 0, (lane_tile(n) + half * (D_FF // LANES)) * LANES))

    return pl.pallas_call(
        functools.partial(_ffn_up_kernel, TM_BIG, overlap),
        out_shape=jax.ShapeDtypeStruct((T, nt * tn), BF16),
        grid=(nt, T // tm),
        in_specs=[
            pl.BlockSpec((tm, D_MODEL), lambda n, m: (m, 0)),
            cols(D_MODEL, 0), cols(D_MODEL, 1),
            cols(CONV_W, 0), cols(CONV_W, 1),
            cols(1, 0), cols(1, 1),
        ],
        out_specs=pl.BlockSpec((tm, tn), lambda n, m: (m, n)),
        scratch_shapes=[pltpu.VMEM((D_MODEL, tn), BF16), pltpu.VMEM((D_MODEL, tn), BF16)],
        compiler_params=_params("arbitrary", "arbitrary"),
        name="ffn_up_conv_geglu",
    )(hb, w_up, w_up, conv_w, conv_w, conv_b, conv_b)


def _const(a):
    return jnp.asarray(np.ascontiguousarray(a, dtype=np.float32))


def _rotary_tables(L):
    pos = (np.arange(L) - PAD).astype(np.float64)
    a_freq = 1.0 / (ROPE_THETA ** (np.arange(0, A_HEAD_DIM, 2, dtype=np.float64) / A_HEAD_DIM))
    ang = pos[:, None] * a_freq[None, :]
    ang = np.concatenate([ang] * (LANES // (A_HEAD_DIM // 2)), axis=-1)
    lane = np.arange(LANES)
    sign_a = np.where((lane & (A_HEAD_DIM // 2)) == 0, -1.0, 1.0)
    cos_a, sin_a = np.cos(ang), np.sin(ang) * sign_a
    scale = (A_HEAD_DIM ** -0.5) * math.log2(math.e)
    cos_a = np.stack([cos_a * scale, cos_a])
    sin_a = np.stack([sin_a * scale, sin_a])

    r_freq = 1.0 / (ROPE_THETA ** np.linspace(0.0, 1.0, R_QK_DIM // 2, dtype=np.float64))
    ang = pos[:, None] * r_freq[None, :]
    ang = np.concatenate([ang, ang], axis=-1)
    sign_r = np.where(lane < R_QK_DIM // 2, -1.0, 1.0)
    cos_r = np.broadcast_to(np.cos(ang), (2, L, LANES))
    sin_r = np.broadcast_to(np.sin(ang) * sign_r, (2, L, LANES))
    valid = (np.arange(L) >= PAD).astype(np.float64)
    post_k = np.broadcast_to(((R_QK_DIM ** -0.5) * valid)[:, None], (L, LANES))
    post_r = np.stack([np.ones((L, LANES)), post_k])
    return (_const(cos_a), _const(sin_a)), (_const(cos_r), _const(sin_r), _const(post_r))


def _retention_tables():
    log_gamma = np.log(1.0 - 2.0 ** (-5.0 - np.arange(R_HEADS, dtype=np.float64)))
    p = np.arange(CHUNK, dtype=np.float64)
    diff = p[:, None] - p[None, :]
    lg = log_gamma[:, None, None]
    decay = np.where(diff >= 0, np.exp(lg * np.maximum(diff, 0.0)), 0.0)
    tail = np.exp(log_gamma[:, None] * (CHUNK - 1 - p))
    qdec = np.exp(log_gamma[:, None] * (p + 1.0))
    tail = np.broadcast_to(tail[:, :, None], (R_HEADS, CHUNK, R_QK_DIM))
    qdec = np.broadcast_to(qdec[:, :, None], (R_HEADS, CHUNK, R_QK_DIM))
    cdec = np.broadcast_to(np.exp(log_gamma * CHUNK)[:, None, None], (R_HEADS, 1, R_V_DIM))
    return _const(decay), _const(tail), _const(qdec), _const(cdec)


def kernel(x, meta_tokens, ln_emb_g, ln_emb_b, w_in, lam_q1, lam_k1, lam_q2, lam_k2, subln_g, ret_gn_g,
           w_branch_a, w_branch_b, w_out, ln1_g, ln1_b, w_up, conv_w, conv_b, w_down, ln2_g, ln2_b):
    B, seq, _ = x.shape
    nb = 1 + seq // CHUNK
    L = nb * CHUNK
    rot_a, rot_r = _rotary_tables(L)
    ret_tabs = _retention_tables()

    conv_b3 = conv_b.reshape(DEPTH, 1, 2 * D_FF)

    hf, hb = _embed_ln(x, meta_tokens, ln_emb_g, ln_emb_b, nb)
    for l in range(DEPTH):
        qk = _proj(hb, w_in, l, C_AQ, 2 * A_QK, "rot_a", BF16, rot_a, lambda n: n // (A_QK // TN_PROJ))
        av = _proj(hb, w_in, l, C_AV, A_VW, "plain", BF16)
        rqk = _proj(hb, w_in, l, C_RQ, 2 * R_QK, "rot_r", F32, rot_r, lambda n: n // (R_QK // TN_PROJ))
        rv = _proj(hb, w_in, l, C_RV, R_VW, "plain", BF16)
        gates = _proj(hb, w_in, l, C_RG, R_VW + 2 * D_MODEL, "plain", F32)

        lamp = jnp.stack([lam_q1[l], lam_k1[l], lam_q2[l], lam_k2[l]])
        oa = _attention(qk, av, lamp, subln_g[l], l, B, L)
        ob = _retention(rqk, rv, gates, ret_gn_g[l], ret_tabs, B, L)
        merged = _merge(oa, ob, w_branch_a, w_branch_b, gates, l)
        hf, hb = _proj_ln(merged, D_MODEL, w_out, l, hf, ln1_g, ln1_b, TM_SMALL, 512, "out_proj_ln")

        gact = _ffn_up(hb, w_up, conv_w, conv_b3, l, L)
        if l + 1 < DEPTH:
            hf, hb = _proj_ln(gact, D_FF, w_down, l, hf, ln2_g, ln2_b, TM_SMALL // 2, 128, "ffn_down_ln")
        else:
            out = _proj_ln_final(gact, D_FF, w_down, l, hf, ln2_g, ln2_b, 2 * CHUNK, 128, nb, "ffn_down_ln_out")
    return out.reshape(B, seq, D_MODEL)
```

```python
import functools
import math

import jax
import jax.numpy as jnp
import numpy as np
from jax import lax
from jax.experimental import pallas as pl
from jax.experimental.pallas import tpu as pltpu

F32 = jnp.float32
BF16 = jnp.bfloat16

D_MODEL = 2048
DEPTH = 2
N_META = 16
CHUNK = 128
PAD = CHUNK - N_META
A_HEAD_DIM = 64
A_V_DIM = 2 * A_HEAD_DIM
A_HEADS = D_MODEL // A_V_DIM
A_QK = A_HEADS * 2 * A_HEAD_DIM
A_VW = A_HEADS * A_V_DIM
R_QK_DIM = 128
R_V_DIM = 2 * R_QK_DIM
R_HEADS = D_MODEL // R_V_DIM
R_QK = R_HEADS * R_QK_DIM
R_VW = R_HEADS * R_V_DIM
D_FF = 128 * ((8 * D_MODEL // 3 + 127) // 128)
CONV_W = 3
ROPE_THETA = 10000.0
ALPHA = (2 * DEPTH) ** 0.25
EPS = 1e-5
NEG = -1e30

LANES = 128
VMEM_LIMIT = 56 * 1024 * 1024

C_AQ = 0
C_AK = C_AQ + A_QK
C_AV = C_AK + A_QK
C_RQ = C_AV + A_VW
C_RK = C_RQ + R_QK
C_RV = C_RK + R_QK
C_RG = C_RV + R_VW
C_GA = C_RG + R_VW
C_GB = C_GA + D_MODEL
W_IN_COLS = C_GB + D_MODEL

TM_BIG = 1088
TM_SMALL = 544
TN_PROJ = 1024
TN_FF = 512
TQ = 256
ROW_SPLIT = 4
MIN_MATMUL_ROWS = 272
HALO = 8


def _params(*sem):
    return pltpu.CompilerParams(dimension_semantics=sem, vmem_limit_bytes=VMEM_LIMIT)


def _layer_norm_rows(z, g, b):
    mu = jnp.mean(z, axis=-1, keepdims=True)
    d = z - mu
    var = jnp.mean(d * d, axis=-1, keepdims=True)
    return d * lax.rsqrt(var + EPS) * g + b


def _sigmoid(x):
    return 1.0 / (1.0 + jnp.exp(-x))


def _software_pipeline(n, produce, consume):
    nxt = produce(0)
    for i in range(n):
        cur = nxt
        if i + 1 < n:
            nxt = produce(i + 1)
        consume(i, cur)


def _embed_ln_kernel(x_ref, meta_ref, g_ref, b_ref, hf_ref, hb_ref):
    g = g_ref[...]
    b = b_ref[...]
    tm = hf_ref.shape[0]

    def emit(rows):
        y = _layer_norm_rows(rows, g, b)
        hf_ref[...] = y
        hb_ref[...] = y.astype(BF16)

    @pl.when(pl.program_id(1) == 0)
    def _():
        emit(jnp.concatenate([jnp.zeros((PAD, D_MODEL), F32), meta_ref[...], x_ref[0:tm - CHUNK, :]], axis=0))

    @pl.when(pl.program_id(1) > 0)
    def _():
        emit(x_ref[...])


def _embed_ln(x, meta, g, b, nb):
    B, seq, _ = x.shape
    L = nb * CHUNK
    tm = TM_SMALL
    per_seq = L // tm
    step = 32
    assert tm % step == 0 and CHUNK % step == 0 and L % tm == 0

    def token_rows(bi, i):
        return (bi * (seq // step) + jnp.maximum(i * (tm // step) - CHUNK // step, 0)) * step, 0

    return pl.pallas_call(
        _embed_ln_kernel,
        out_shape=(jax.ShapeDtypeStruct((B * L, D_MODEL), F32), jax.ShapeDtypeStruct((B * L, D_MODEL), BF16)),
        grid=(B, per_seq),
        in_specs=[
            pl.BlockSpec((pl.Element(tm), pl.Element(D_MODEL)), token_rows),
            pl.BlockSpec((N_META, D_MODEL), lambda bi, i: (0, 0)),
            pl.BlockSpec((1, D_MODEL), lambda bi, i: (0, 0)),
            pl.BlockSpec((1, D_MODEL), lambda bi, i: (0, 0)),
        ],
        out_specs=(
            pl.BlockSpec((tm, D_MODEL), lambda bi, i: (bi * per_seq + i, 0)),
            pl.BlockSpec((tm, D_MODEL), lambda bi, i: (bi * per_seq + i, 0)),
        ),
        compiler_params=_params("arbitrary", "arbitrary"),
        name="embed_ln",
    )(x.reshape(B * seq, D_MODEL), meta, g.reshape(1, D_MODEL), b.reshape(1, D_MODEL))


def _proj_kernel(kind, h_ref, w_ref, *rest):
    if kind == "plain":
        o_ref, wb_ref = rest
    elif kind == "rot_a":
        cos_ref, sin_ref, o_ref, wb_ref = rest
    else:
        cos_ref, sin_ref, post_ref, o_ref, wb_ref = rest

    @pl.when(pl.program_id(1) == 0)
    def _():
        wb_ref[...] = w_ref[...].astype(BF16)

    tm, tn = o_ref.shape
    rb = tm // ROW_SPLIT
    if kind == "rot_a":
        lane = lax.broadcasted_iota(jnp.int32, (rb, LANES), 1)
        low_half = (lane & (A_HEAD_DIM // 2)) == 0

    def matmul(r):
        return jnp.dot(h_ref[r * rb:(r + 1) * rb, :], wb_ref[...], preferred_element_type=F32)

    def epilogue(r, acc):
        rows = slice(r * rb, (r + 1) * rb)
        if kind == "plain":
            o_ref[rows, :] = acc.astype(o_ref.dtype)
            return
        cos = cos_ref[rows, :]
        sin = sin_ref[rows, :]
        for j in range(tn // LANES):
            t = acc[:, j * LANES:(j + 1) * LANES]
            if kind == "rot_a":
                rot = jnp.where(low_half, pltpu.roll(t, LANES - A_HEAD_DIM // 2, 1),
                                pltpu.roll(t, A_HEAD_DIM // 2, 1))
                y = t * cos + rot * sin
            else:
                y = (t * cos + pltpu.roll(t, R_QK_DIM // 2, 1) * sin) * post_ref[rows, :]
            o_ref[rows, j * LANES:(j + 1) * LANES] = y.astype(o_ref.dtype)

    _software_pipeline(ROW_SPLIT, matmul, epilogue)


def _proj(hb, w_in, layer, col0, ncols, kind, out_dtype, tables=(), variant=None, tm=TM_BIG):
    T = hb.shape[0]
    tn = TN_PROJ
    nt, mt = ncols // tn, T // tm
    tile0 = col0 // tn
    assert not tables or tm == TM_BIG
    in_specs = [
        pl.BlockSpec((tm, D_MODEL), lambda n, m: (m, 0)),
        pl.BlockSpec((None, D_MODEL, tn), lambda n, m: (layer, 0, tile0 + n)),
    ]
    for _ in tables:
        in_specs.append(pl.BlockSpec((None, tm, LANES), lambda n, m: (variant(n), m % 2, 0)))
    return pl.pallas_call(
        functools.partial(_proj_kernel, kind),
        out_shape=jax.ShapeDtypeStruct((T, ncols), out_dtype),
        grid=(nt, mt),
        in_specs=in_specs,
        out_specs=pl.BlockSpec((tm, tn), lambda n, m: (m, n)),
        scratch_shapes=[pltpu.VMEM((D_MODEL, tn), BF16)],
        compiler_params=_params("arbitrary", "arbitrary"),
        name="proj_" + kind,
    )(hb, w_in, *tables)


def _attn_kernel(lam_init, q_blocks, q_ref, k_ref, v_ref, lamp_ref, g_ref, o_ref):
    lp = lamp_ref[...]
    lam = (jnp.exp(jnp.sum(lp[0:1] * lp[1:2], axis=-1, keepdims=True))
           - jnp.exp(jnp.sum(lp[2:3] * lp[3:4], axis=-1, keepdims=True)) + lam_init)
    lane = lax.broadcasted_iota(jnp.int32, (1, LANES), 1)
    map1 = jnp.where(lane < A_HEAD_DIM, 1.0, 0.0).astype(BF16)
    map2 = jnp.where(lane >= A_HEAD_DIM, 1.0, 0.0).astype(BF16)
    g = g_ref[...]
    nt = (((1,), (1,)), ((), ()))

    def scores(r0, r1):
        q = q_ref[r0:r1, :]
        return lax.dot_general(jnp.concatenate([q * map1, q * map2], axis=0), k_ref[0:r1, :], nt,
                               preferred_element_type=F32)

    def weights(r0, r1, s12):
        tq = r1 - r0
        subs = [sub_weights(r0 + t, r1, s12[t:t + CHUNK], s12[tq + t:tq + t + CHUNK])
                for t in range(0, tq, CHUNK)]
        if len(subs) == 1:
            return subs[0]
        return tuple(jnp.concatenate(parts, axis=0) for parts in zip(*subs))

    def sub_weights(r0, r_end, s1, s2):
        keys = r0 + CHUNK
        qq = r0 + lax.broadcasted_iota(jnp.int32, (CHUNK, CHUNK), 0)
        kk = lax.broadcasted_iota(jnp.int32, (CHUNK, CHUNK), 1)

        def masked_probs(s):
            parts = []
            for c0 in range(0, keys, CHUNK):
                piece = s[:, c0:c0 + CHUNK]
                cond = None
                if c0 == 0:
                    cond = kk >= PAD
                    if r0 == 0:
                        cond = cond | (kk == qq)
                if c0 + CHUNK > r0:
                    causal = (kk + c0) <= qq
                    cond = causal if cond is None else (cond & causal)
                if cond is not None:
                    piece = jnp.where(cond, piece, NEG)
                parts.append(piece)
            s = parts[0] if len(parts) == 1 else jnp.concatenate(parts, axis=1)
            p = jnp.exp2(s - jnp.max(s, axis=-1, keepdims=True))
            return p, jnp.sum(p, axis=-1, keepdims=True)

        p1, l1 = masked_probs(s1)
        p2, l2 = masked_probs(s2)
        a = (p1 - p2 * (lam * l1 * (1.0 / l2))).astype(BF16)
        if r_end > keys:
            a = jnp.concatenate([a, jnp.zeros((CHUNK, r_end - keys), BF16)], axis=1)
        return a, 1.0 / l1

    def values(r0, r1, a, inv_l1):
        o = jnp.dot(a, v_ref[0:r1, :], preferred_element_type=F32) * inv_l1
        y = o * lax.rsqrt(jnp.mean(o * o, axis=-1, keepdims=True) + EPS) * g
        o_ref[r0:r1, :] = (y * (1.0 - lam_init)).astype(BF16)

    nq = len(q_blocks)
    s_live, a_live = {}, {}
    for i in range(nq + 2):
        if i < nq:
            s_live[i] = scores(*q_blocks[i])
        if i >= 2:
            values(*q_blocks[i - 2], *a_live.pop(i - 2))
        if 1 <= i <= nq:
            a_live[i - 1] = weights(*q_blocks[i - 1], s_live.pop(i - 1))


def _attention(qk, v, lamp, subln_g, layer, B, L):
    T = B * L
    lam_init = 0.8 - 0.6 * math.exp(-0.3 * layer)
    q_blocks = ((0, CHUNK),) + tuple((r, r + TQ) for r in range(CHUNK, L, TQ))
    kh = A_QK // LANES
    return pl.pallas_call(
        functools.partial(_attn_kernel, lam_init, q_blocks),
        out_shape=jax.ShapeDtypeStruct((T, A_VW), BF16),
        grid=(B, A_HEADS),
        in_specs=[
            pl.BlockSpec((L, LANES), lambda b, h: (b, h)),
            pl.BlockSpec((L, LANES), lambda b, h: (b, kh + h)),
            pl.BlockSpec((L, A_V_DIM), lambda b, h: (b, h)),
            pl.BlockSpec((4, A_HEAD_DIM), lambda b, h: (0, 0)),
            pl.BlockSpec((1, A_V_DIM), lambda b, h: (0, 0)),
        ],
        out_specs=pl.BlockSpec((L, A_V_DIM), lambda b, h: (b, h)),
        compiler_params=_params("arbitrary", "arbitrary"),
        name="diff_attention",
    )(qk, qk, v, lamp, subln_g.reshape(1, A_V_DIM))


def _retention_kernel(nc, q_ref, k_ref, v_ref, rg_ref, gn_ref, decay_ref, tail_ref, qdec_ref,
                      cdec_ref, o_ref):
    decay = decay_ref[...]
    tail = tail_ref[...]
    qdec = qdec_ref[...]
    cdec = cdec_ref[...]
    gn = gn_ref[...]
    nt = (((1,), (1,)), ((), ()))
    tn = (((0,), (0,)), ((), ()))
    state = None
    for c in range(nc):
        rows = slice(c * CHUNK, (c + 1) * CHUNK)
        qc = q_ref[rows, :]
        kc = k_ref[rows, :]
        vc = v_ref[rows, :]
        s = lax.dot_general(qc.astype(BF16), kc.astype(BF16), nt, preferred_element_type=F32) * decay
        o = jnp.dot(s.astype(BF16), vc, preferred_element_type=F32)
        if state is not None:
            o = o + jnp.dot((qc * qdec).astype(BF16), state.astype(BF16), preferred_element_type=F32)
        if c + 1 < nc:
            kv = lax.dot_general((kc * tail).astype(BF16), vc, tn, preferred_element_type=F32)
            state = kv if state is None else cdec * state + kv
        mu = jnp.mean(o, axis=-1, keepdims=True)
        d = o - mu
        var = jnp.mean(d * d, axis=-1, keepdims=True)
        y = d * lax.rsqrt(var + EPS) * gn
        rg = rg_ref[rows, :]
        o_ref[rows, :] = (y * (rg * _sigmoid(rg))).astype(BF16)


def _retention(rqk, rv, gates, gn_g, tabs, B, L):
    T = B * L
    decay, tail, qdec, cdec = tabs
    kh = R_QK // R_QK_DIM
    return pl.pallas_call(
        functools.partial(_retention_kernel, L // CHUNK),
        out_shape=jax.ShapeDtypeStruct((T, R_VW), BF16),
        grid=(B, R_HEADS),
        in_specs=[
            pl.BlockSpec((L, R_QK_DIM), lambda b, h: (b, h)),
            pl.BlockSpec((L, R_QK_DIM), lambda b, h: (b, kh + h)),
            pl.BlockSpec((L, R_V_DIM), lambda b, h: (b, h)),
            pl.BlockSpec((L, R_V_DIM), lambda b, h: (b, h)),
            pl.BlockSpec((1, R_V_DIM), lambda b, h: (0, h)),
            pl.BlockSpec((None, CHUNK, CHUNK), lambda b, h: (h, 0, 0)),
            pl.BlockSpec((None, CHUNK, R_QK_DIM), lambda b, h: (h, 0, 0)),
            pl.BlockSpec((None, CHUNK, R_QK_DIM), lambda b, h: (h, 0, 0)),
            pl.BlockSpec((None, 1, R_V_DIM), lambda b, h: (h, 0, 0)),
        ],
        out_specs=pl.BlockSpec((L, R_V_DIM), lambda b, h: (b, h)),
        compiler_params=_params("arbitrary", "arbitrary"),
        name="retention",
    )(rqk, rqk, rv, gates, gn_g.reshape(1, R_VW), decay, tail, qdec, cdec)


def _merge_kernel(n_chunks, oa_ref, ob_ref, wa_ref, wb_ref, ga_ref, gb_ref, o_ref, was_ref, wbs_ref):
    s = pl.program_id(0)
    ck = wa_ref.shape[1]

    @pl.when(s < n_chunks)
    def _():
        col = pl.multiple_of(s * ck, ck)
        was_ref[:, pl.ds(col, ck)] = wa_ref[...].astype(BF16)

    @pl.when(jnp.logical_and(s >= n_chunks, s < 2 * n_chunks))
    def _():
        col = pl.multiple_of((s - n_chunks) * ck, ck)
        wbs_ref[:, pl.ds(col, ck)] = wb_ref[...].astype(BF16)

    @pl.when(s >= 2 * n_chunks)
    def _():
        def matmul(i):
            x_ref, ws_ref = ((oa_ref, was_ref), (ob_ref, wbs_ref))[i]
            return jnp.dot(x_ref[...], ws_ref[...], preferred_element_type=F32)

        gated = []

        def gate(i, branch):
            gated.append(_sigmoid((ga_ref, gb_ref)[i][...]) * branch)

        _software_pipeline(2, matmul, gate)
        o_ref[...] = (gated[0] + gated[1]).astype(BF16)


def _merge(oa, ob, w_a, w_b, gates, layer):
    T = oa.shape[0]
    tm, ck = TM_SMALL // 2, 512
    nck = D_MODEL // ck
    row = lambda s: (jnp.maximum(s - 2 * nck, 0), 0)
    ga0 = (C_GA - C_RG) // D_MODEL
    gb0 = (C_GB - C_RG) // D_MODEL
    return pl.pallas_call(
        functools.partial(_merge_kernel, nck),
        out_shape=jax.ShapeDtypeStruct((T, D_MODEL), BF16),
        grid=(2 * nck + T // tm,),
        in_specs=[
            pl.BlockSpec((tm, A_VW), row),
            pl.BlockSpec((tm, R_VW), row),
            pl.BlockSpec((None, A_VW, ck), lambda s: (layer, 0, jnp.minimum(s, nck - 1))),
            pl.BlockSpec((None, R_VW, ck), lambda s: (layer, 0, jnp.clip(s - nck, 0, nck - 1))),
            pl.BlockSpec((tm, D_MODEL), lambda s: (jnp.maximum(s - 2 * nck, 0), ga0)),
            pl.BlockSpec((tm, D_MODEL), lambda s: (jnp.maximum(s - 2 * nck, 0), gb0)),
        ],
        out_specs=pl.BlockSpec((tm, D_MODEL), row),
        scratch_shapes=[pltpu.VMEM((A_VW, D_MODEL), BF16), pltpu.VMEM((R_VW, D_MODEL), BF16)],
        compiler_params=_params("arbitrary"),
        name="merge",
    )(oa, ob, w_a, w_b, gates, gates)


def _proj_ln_kernel(n_chunks, row_split, a_ref, w_ref, h_ref, g_ref, b_ref, of_ref, *rest):
    ob_ref, ws_ref = rest if len(rest) == 2 else (None, rest[0])
    s = pl.program_id(0)
    ck = w_ref.shape[1]

    @pl.when(s < n_chunks)
    def _():
        col = pl.multiple_of(s * ck, ck)
        ws_ref[:, pl.ds(col, ck)] = w_ref[...].astype(BF16)

    @pl.when(s >= n_chunks)
    def _():
        rb = a_ref.shape[0] // row_split

        def matmul(r):
            return jnp.dot(a_ref[r * rb:(r + 1) * rb, :], ws_ref[...], preferred_element_type=F32)

        def residual_ln(r, acc):
            rows = slice(r * rb, (r + 1) * rb)
            y = _layer_norm_rows(ALPHA * h_ref[rows, :] + acc, g_ref[...], b_ref[...])
            of_ref[rows, :] = y
            if ob_ref is not None:
                ob_ref[rows, :] = y.astype(BF16)

        _software_pipeline(row_split, matmul, residual_ln)


def _proj_ln(a, kdim, w, layer, hf, g, b, tm, ck, name):
    T = hf.shape[0]
    nck = D_MODEL // ck
    row = lambda s: (jnp.maximum(s - nck, 0), 0)
    return pl.pallas_call(
        functools.partial(_proj_ln_kernel, nck, tm // MIN_MATMUL_ROWS),
        out_shape=(jax.ShapeDtypeStruct((T, D_MODEL), F32), jax.ShapeDtypeStruct((T, D_MODEL), BF16)),
        grid=(nck + T // tm,),
        in_specs=[
            pl.BlockSpec((tm, kdim), row),
            pl.BlockSpec((None, kdim, ck), lambda s: (layer, 0, jnp.minimum(s, nck - 1))),
            pl.BlockSpec((tm, D_MODEL), row),
            pl.BlockSpec((None, 1, D_MODEL), lambda s: (layer, 0, 0)),
            pl.BlockSpec((None, 1, D_MODEL), lambda s: (layer, 0, 0)),
        ],
        out_specs=(pl.BlockSpec((tm, D_MODEL), row), pl.BlockSpec((tm, D_MODEL), row)),
        scratch_shapes=[pltpu.VMEM((kdim, D_MODEL), BF16)],
        compiler_params=_params("arbitrary"),
        name=name,
    )(a, w, hf, g.reshape(DEPTH, 1, D_MODEL), b.reshape(DEPTH, 1, D_MODEL))


def _proj_ln_final(a, kdim, w, layer, hf, g, b, tm, ck, nb, name):
    T = hf.shape[0]
    B = T // (nb * CHUNK)
    seq = (nb - 1) * CHUNK
    nck = D_MODEL // ck
    per_seq = seq // tm

    def rows(cols):
        def index(s):
            t = jnp.maximum(s - nck, 0)
            return ((t // per_seq) * nb + 1 + (t % per_seq) * (tm // CHUNK)) * CHUNK, 0
        return pl.BlockSpec((pl.Element(tm), pl.Element(cols)), index)

    return pl.pallas_call(
        functools.partial(_proj_ln_kernel, nck, 1),
        out_shape=jax.ShapeDtypeStruct((B * seq, D_MODEL), F32),
        grid=(nck + B * per_seq,),
        in_specs=[
            rows(kdim),
            pl.BlockSpec((None, kdim, ck), lambda s: (layer, 0, jnp.minimum(s, nck - 1))),
            rows(D_MODEL),
            pl.BlockSpec((None, 1, D_MODEL), lambda s: (layer, 0, 0)),
            pl.BlockSpec((None, 1, D_MODEL), lambda s: (layer, 0, 0)),
        ],
        out_specs=pl.BlockSpec((tm, D_MODEL), lambda s: (jnp.maximum(s - nck, 0), 0)),
        scratch_shapes=[pltpu.VMEM((kdim, D_MODEL), BF16)],
        compiler_params=_params("arbitrary"),
        name=name,
    )(a, w, hf, g.reshape(DEPTH, 1, D_MODEL), b.reshape(DEPTH, 1, D_MODEL))


def _ffn_up_kernel(rb, overlap, h_ref, wg_ref, wv_ref, cwg_ref, cwv_ref, cbg_ref, cbv_ref, o_ref,
                   wgs_ref, wvs_ref):
    tn = o_ref.shape[1]

    @pl.when(pl.program_id(1) == 0)
    def _():
        wgs_ref[...] = wg_ref[...].astype(BF16)
        wvs_ref[...] = wv_ref[...].astype(BF16)

    row8 = lax.broadcasted_iota(jnp.int32, (HALO, tn), 0)
    halves = ((wgs_ref, cwg_ref, cbg_ref), (wvs_ref, cwv_ref, cbv_ref))
    prevs = [jnp.zeros((HALO, tn), F32)] * 2
    gates = []

    def matmul(i):
        r, half = divmod(i, 2)
        u = jnp.dot(h_ref[r * rb:(r + 1) * rb, :], halves[half][0][...], preferred_element_type=F32)
        if r == 0:
            u = jnp.concatenate([jnp.zeros((PAD, tn), F32), u[PAD:]], axis=0)
        return u

    def shifted(u, prev, k):
        rolled = pltpu.roll(u, k, 0)
        head = rolled[:HALO]
        for t in range(k):
            head = jnp.where(row8 == t, prev[HALO - k + t:HALO - k + t + 1], head)
        return jnp.concatenate([head, rolled[HALO:]], axis=0)

    def conv_geglu(i, u):
        r, half = divmod(i, 2)
        _, cw_ref, cb_ref = halves[half]
        prev = prevs[half]
        prevs[half] = u[rb - HALO:rb, :]
        scale = 1.0 if half == 0 else 0.5
        cw = cw_ref[...] * scale
        c = shifted(u, prev, 2) * cw[0:1] + shifted(u, prev, 1) * cw[1:2] + u * cw[2:3] + cb_ref[...] * scale
        if half == 0:
            gates.append(c)
        else:
            gate = gates.pop()
            y = gate * (1.0 + lax.erf(gate * (2.0 ** -0.5))) * c
            moved = jnp.concatenate([y[:, overlap:], jnp.zeros((rb, overlap), F32)], axis=1)
            y = jnp.where(pl.program_id(0) == pl.num_programs(0) - 1, moved, y)
            o_ref[r * rb:(r + 1) * rb, :] = y.astype(BF16)

    _software_pipeline(2 * (h_ref.shape[0] // rb), matmul, conv_geglu)


def _ffn_up(hb, w_up, conv_w, conv_b, layer, L):
    T = hb.shape[0]
    tm, tn = L, TN_FF
    assert PAD <= TM_BIG and L % TM_BIG == 0
    nt = pl.cdiv(D_FF, tn)
    overlap = nt * tn - D_FF
    lane_tile = lambda n: jnp.minimum(n * (tn // LANES), (D_FF - tn) // LANES)

    def cols(rows, half):
        return pl.BlockSpec((None, pl.Element(rows), pl.Element(tn)),
                            lambda n, m: (layer, 0, (lane_tile(n) + half * (D_FF // LANES)) * LANES))

    return pl.pallas_call(
        functools.partial(_ffn_up_kernel, TM_BIG, overlap),
        out_shape=jax.ShapeDtypeStruct((T, nt * tn), BF16),
        grid=(nt, T // tm),
        in_specs=[
            pl.BlockSpec((tm, D_MODEL), lambda n, m: (m, 0)),
            cols(D_MODEL, 0), cols(D_MODEL, 1),
            cols(CONV_W, 0), cols(CONV_W, 1),
            cols(1, 0), cols(1, 1),
        ],
        out_specs=pl.BlockSpec((tm, tn), lambda n, m: (m, n)),
        scratch_shapes=[pltpu.VMEM((D_MODEL, tn), BF16), pltpu.VMEM((D_MODEL, tn), BF16)],
        compiler_params=_params("arbitrary", "arbitrary"),
        name="ffn_up_conv_geglu",
    )(hb, w_up, w_up, conv_w, conv_w, conv_b, conv_b)


def _const(a):
    return jnp.asarray(np.ascontiguousarray(a, dtype=np.float32))


def _rotary_tables(L):
    pos = (np.arange(L) - PAD).astype(np.float64)
    a_freq = 1.0 / (ROPE_THETA ** (np.arange(0, A_HEAD_DIM, 2, dtype=np.float64) / A_HEAD_DIM))
    ang = pos[:, None] * a_freq[None, :]
    ang = np.concatenate([ang] * (LANES // (A_HEAD_DIM // 2)), axis=-1)
    lane = np.arange(LANES)
    sign_a = np.where((lane & (A_HEAD_DIM // 2)) == 0, -1.0, 1.0)
    cos_a, sin_a = np.cos(ang), np.sin(ang) * sign_a
    scale = (A_HEAD_DIM ** -0.5) * math.log2(math.e)
    cos_a = np.stack([cos_a * scale, cos_a])
    sin_a = np.stack([sin_a * scale, sin_a])

    r_freq = 1.0 / (ROPE_THETA ** np.linspace(0.0, 1.0, R_QK_DIM // 2, dtype=np.float64))
    ang = pos[:, None] * r_freq[None, :]
    ang = np.concatenate([ang, ang], axis=-1)
    sign_r = np.where(lane < R_QK_DIM // 2, -1.0, 1.0)
    cos_r = np.broadcast_to(np.cos(ang), (2, L, LANES))
    sin_r = np.broadcast_to(np.sin(ang) * sign_r, (2, L, LANES))
    valid = (np.arange(L) >= PAD).astype(np.float64)
    post_k = np.broadcast_to(((R_QK_DIM ** -0.5) * valid)[:, None], (L, LANES))
    post_r = np.stack([np.ones((L, LANES)), post_k])
    return (_const(cos_a), _const(sin_a)), (_const(cos_r), _const(sin_r), _const(post_r))


def _retention_tables():
    log_gamma = np.log(1.0 - 2.0 ** (-5.0 - np.arange(R_HEADS, dtype=np.float64)))
    p = np.arange(CHUNK, dtype=np.float64)
    diff = p[:, None] - p[None, :]
    lg = log_gamma[:, None, None]
    decay = np.where(diff >= 0, np.exp(lg * np.maximum(diff, 0.0)), 0.0)
    tail = np.exp(log_gamma[:, None] * (CHUNK - 1 - p))
    qdec = np.exp(log_gamma[:, None] * (p + 1.0))
    tail = np.broadcast_to(tail[:, :, None], (R_HEADS, CHUNK, R_QK_DIM))
    qdec = np.broadcast_to(qdec[:, :, None], (R_HEADS, CHUNK, R_QK_DIM))
    cdec = np.broadcast_to(np.exp(log_gamma * CHUNK)[:, None, None], (R_HEADS, 1, R_V_DIM))
    return _const(decay), _const(tail), _const(qdec), _const(cdec)


def kernel(x, meta_tokens, ln_emb_g, ln_emb_b, w_in, lam_q1, lam_k1, lam_q2, lam_k2, subln_g, ret_gn_g,
           w_branch_a, w_branch_b, w_out, ln1_g, ln1_b, w_up, conv_w, conv_b, w_down, ln2_g, ln2_b):
    B, seq, _ = x.shape
    nb = 1 + seq // CHUNK
    L = nb * CHUNK
    rot_a, rot_r = _rotary_tables(L)
    ret_tabs = _retention_tables()

    conv_b3 = conv_b.reshape(DEPTH, 1, 2 * D_FF)

    hf, hb = _embed_ln(x, meta_tokens, ln_emb_g, ln_emb_b, nb)
    for l in range(DEPTH):
        qk = _proj(hb, w_in, l, C_AQ, 2 * A_QK, "rot_a", BF16, rot_a, lambda n: n // (A_QK // TN_PROJ))
        av = _proj(hb, w_in, l, C_AV, A_VW, "plain", BF16, tm=L)
        rqk = _proj(hb, w_in, l, C_RQ, 2 * R_QK, "rot_r", F32, rot_r, lambda n: n // (R_QK // TN_PROJ))
        rv = _proj(hb, w_in, l, C_RV, R_VW, "plain", BF16, tm=L)
        gates = _proj(hb, w_in, l, C_RG, R_VW + 2 * D_MODEL, "plain", F32)

        lamp = jnp.stack([lam_q1[l], lam_k1[l], lam_q2[l], lam_k2[l]])
        oa = _attention(qk, av, lamp, subln_g[l], l, B, L)
        ob = _retention(rqk, rv, gates, ret_gn_g[l], ret_tabs, B, L)
        merged = _merge(oa, ob, w_branch_a, w_branch_b, gates, l)
        hf, hb = _proj_ln(merged, D_MODEL, w_out, l, hf, ln1_g, ln1_b, TM_SMALL, 512, "out_proj_ln")

        gact = _ffn_up(hb, w_up, conv_w, conv_b3, l, L)
        if l + 1 < DEPTH:
            hf, hb = _proj_ln(gact, D_FF, w_down, l, hf, ln2_g, ln2_b, TM_SMALL // 2, 128, "ffn_down_ln")
        else:
            out = _proj_ln_final(gact, D_FF, w_down, l, hf, ln2_g, ln2_b, 2 * CHUNK, 128, nb, "ffn_down_ln_out")
    return out.reshape(B, seq, D_MODEL)
```

```python
import functools
import math

import jax
import jax.numpy as jnp
import numpy as np
from jax import lax
from jax.experimental import pallas as pl
from jax.experimental.pallas import tpu as pltpu

F32 = jnp.float32
BF16 = jnp.bfloat16

D_MODEL = 2048
DEPTH = 2
N_META = 16
CHUNK = 128
PAD = CHUNK - N_META
A_HEAD_DIM = 64
A_V_DIM = 2 * A_HEAD_DIM
A_HEADS = D_MODEL // A_V_DIM
A_QK = A_HEADS * 2 * A_HEAD_DIM
A_VW = A_HEADS * A_V_DIM
R_QK_DIM = 128
R_V_DIM = 2 * R_QK_DIM
R_HEADS = D_MODEL // R_V_DIM
R_QK = R_HEADS * R_QK_DIM
R_VW = R_HEADS * R_V_DIM
D_FF = 128 * ((8 * D_MODEL // 3 + 127) // 128)
CONV_W = 3
ROPE_THETA = 10000.0
ALPHA = (2 * DEPTH) ** 0.25
EPS = 1e-5
NEG = -1e30

LANES = 128
VMEM_LIMIT = 56 * 1024 * 1024

C_AQ = 0
C_AK = C_AQ + A_QK
C_AV = C_AK + A_QK
C_RQ = C_AV + A_VW
C_RK = C_RQ + R_QK
C_RV = C_RK + R_QK
C_RG = C_RV + R_VW
C_GA = C_RG + R_VW
C_GB = C_GA + D_MODEL
W_IN_COLS = C_GB + D_MODEL

TM_BIG = 1088
TM_SMALL = 544
TM_RESIDENT = 272
TM_FINAL = 2 * CHUNK
TN_PROJ = 1024
TN_FF = 512
STAGE_COLS = 512
STAGE_COLS_DOWN = 128
TQ = 256
ROW_SPLIT = 4
MIN_MATMUL_ROWS = 272
HALO = 8


def _params(*sem):
    return pltpu.CompilerParams(dimension_semantics=sem, vmem_limit_bytes=VMEM_LIMIT)


def _layer_norm_rows(z, g, b):
    mu = jnp.mean(z, axis=-1, keepdims=True)
    d = z - mu
    var = jnp.mean(d * d, axis=-1, keepdims=True)
    return d * lax.rsqrt(var + EPS) * g + b


def _sigmoid(x):
    return 1.0 / (1.0 + jnp.exp(-x))


def _software_pipeline(n, produce, consume):
    nxt = produce(0)
    for i in range(n):
        cur = nxt
        if i + 1 < n:
            nxt = produce(i + 1)
        consume(i, cur)


def _embed_ln_kernel(x_ref, meta_ref, g_ref, b_ref, hf_ref, hb_ref):
    g = g_ref[...]
    b = b_ref[...]
    tm = hf_ref.shape[0]

    def emit(rows):
        y = _layer_norm_rows(rows, g, b)
        hf_ref[...] = y
        hb_ref[...] = y.astype(BF16)

    @pl.when(pl.program_id(1) == 0)
    def _():
        emit(jnp.concatenate([jnp.zeros((PAD, D_MODEL), F32), meta_ref[...], x_ref[0:tm - CHUNK, :]], axis=0))

    @pl.when(pl.program_id(1) > 0)
    def _():
        emit(x_ref[...])


def _embed_ln(x, meta, g, b, nb):
    B, seq, _ = x.shape
    L = nb * CHUNK
    tm = TM_SMALL
    per_seq = L // tm
    step = 32
    assert tm % step == 0 and CHUNK % step == 0 and L % tm == 0

    def token_rows(bi, i):
        return (bi * (seq // step) + jnp.maximum(i * (tm // step) - CHUNK // step, 0)) * step, 0

    return pl.pallas_call(
        _embed_ln_kernel,
        out_shape=(jax.ShapeDtypeStruct((B * L, D_MODEL), F32), jax.ShapeDtypeStruct((B * L, D_MODEL), BF16)),
        grid=(B, per_seq),
        in_specs=[
            pl.BlockSpec((pl.Element(tm), pl.Element(D_MODEL)), token_rows),
            pl.BlockSpec((N_META, D_MODEL), lambda bi, i: (0, 0)),
            pl.BlockSpec((1, D_MODEL), lambda bi, i: (0, 0)),
            pl.BlockSpec((1, D_MODEL), lambda bi, i: (0, 0)),
        ],
        out_specs=(
            pl.BlockSpec((tm, D_MODEL), lambda bi, i: (bi * per_seq + i, 0)),
            pl.BlockSpec((tm, D_MODEL), lambda bi, i: (bi * per_seq + i, 0)),
        ),
        compiler_params=_params("arbitrary", "arbitrary"),
        name="embed_ln",
    )(x.reshape(B * seq, D_MODEL), meta, g.reshape(1, D_MODEL), b.reshape(1, D_MODEL))


def _proj_kernel(kind, h_ref, w_ref, *rest):
    if kind == "plain":
        o_ref, wb_ref = rest
    elif kind == "rot_a":
        cos_ref, sin_ref, o_ref, wb_ref = rest
    else:
        cos_ref, sin_ref, post_ref, o_ref, wb_ref = rest

    @pl.when(pl.program_id(1) == 0)
    def _():
        wb_ref[...] = w_ref[...].astype(BF16)

    tm, tn = o_ref.shape
    rb = tm // ROW_SPLIT
    if kind == "rot_a":
        lane = lax.broadcasted_iota(jnp.int32, (rb, LANES), 1)
        low_half = (lane & (A_HEAD_DIM // 2)) == 0

    def matmul(r):
        return jnp.dot(h_ref[r * rb:(r + 1) * rb, :], wb_ref[...], preferred_element_type=F32)

    def epilogue(r, acc):
        rows = slice(r * rb, (r + 1) * rb)
        if kind == "plain":
            o_ref[rows, :] = acc.astype(o_ref.dtype)
            return
        cos = cos_ref[rows, :]
        sin = sin_ref[rows, :]
        for j in range(tn // LANES):
            t = acc[:, j * LANES:(j + 1) * LANES]
            if kind == "rot_a":
                rot = jnp.where(low_half, pltpu.roll(t, LANES - A_HEAD_DIM // 2, 1),
                                pltpu.roll(t, A_HEAD_DIM // 2, 1))
                y = t * cos + rot * sin
            else:
                y = (t * cos + pltpu.roll(t, R_QK_DIM // 2, 1) * sin) * post_ref[rows, :]
            o_ref[rows, j * LANES:(j + 1) * LANES] = y.astype(o_ref.dtype)

    _software_pipeline(ROW_SPLIT, matmul, epilogue)


def _proj(hb, w_in, layer, col0, ncols, kind, out_dtype, tables=(), variant=None):
    T = hb.shape[0]
    tm, tn = TM_BIG, TN_PROJ
    nt, mt = ncols // tn, T // tm
    tile0 = col0 // tn
    in_specs = [
        pl.BlockSpec((tm, D_MODEL), lambda n, m: (m, 0)),
        pl.BlockSpec((None, D_MODEL, tn), lambda n, m: (layer, 0, tile0 + n)),
    ]
    for _ in tables:
        in_specs.append(pl.BlockSpec((None, tm, LANES), lambda n, m: (variant(n), m % 2, 0)))
    return pl.pallas_call(
        functools.partial(_proj_kernel, kind),
        out_shape=jax.ShapeDtypeStruct((T, ncols), out_dtype),
        grid=(nt, mt),
        in_specs=in_specs,
        out_specs=pl.BlockSpec((tm, tn), lambda n, m: (m, n)),
        scratch_shapes=[pltpu.VMEM((D_MODEL, tn), BF16)],
        compiler_params=_params("arbitrary", "arbitrary"),
        name="proj_" + kind,
    )(hb, w_in, *tables)


def _attn_kernel(lam_init, q_blocks, q_ref, k_ref, v_ref, lamp_ref, g_ref, o_ref):
    lp = lamp_ref[...]
    lam = (jnp.exp(jnp.sum(lp[0:1] * lp[1:2], axis=-1, keepdims=True))
           - jnp.exp(jnp.sum(lp[2:3] * lp[3:4], axis=-1, keepdims=True)) + lam_init)
    lane = lax.broadcasted_iota(jnp.int32, (1, LANES), 1)
    map1 = jnp.where(lane < A_HEAD_DIM, 1.0, 0.0).astype(BF16)
    map2 = jnp.where(lane >= A_HEAD_DIM, 1.0, 0.0).astype(BF16)
    g = g_ref[...]
    nt = (((1,), (1,)), ((), ()))

    def scores(r0, r1):
        q = q_ref[r0:r1, :]
        return lax.dot_general(jnp.concatenate([q * map1, q * map2], axis=0), k_ref[0:r1, :], nt,
                               preferred_element_type=F32)

    def weights(r0, r1, s12):
        tq = r1 - r0
        subs = [sub_weights(r0 + t, r1, s12[t:t + CHUNK], s12[tq + t:tq + t + CHUNK])
                for t in range(0, tq, CHUNK)]
        if len(subs) == 1:
            return subs[0]
        return tuple(jnp.concatenate(parts, axis=0) for parts in zip(*subs))

    def sub_weights(r0, r_end, s1, s2):
        keys = r0 + CHUNK
        qq = r0 + lax.broadcasted_iota(jnp.int32, (CHUNK, CHUNK), 0)
        kk = lax.broadcasted_iota(jnp.int32, (CHUNK, CHUNK), 1)

        def masked_probs(s):
            parts = []
            for c0 in range(0, keys, CHUNK):
                piece = s[:, c0:c0 + CHUNK]
                cond = None
                if c0 == 0:
                    cond = kk >= PAD
                    if r0 == 0:
                        cond = cond | (kk == qq)
                if c0 + CHUNK > r0:
                    causal = (kk + c0) <= qq
                    cond = causal if cond is None else (cond & causal)
                if cond is not None:
                    piece = jnp.where(cond, piece, NEG)
                parts.append(piece)
            s = parts[0] if len(parts) == 1 else jnp.concatenate(parts, axis=1)
            p = jnp.exp2(s - jnp.max(s, axis=-1, keepdims=True))
            return p, jnp.sum(p, axis=-1, keepdims=True)

        p1, l1 = masked_probs(s1)
        p2, l2 = masked_probs(s2)
        a = (p1 - p2 * (lam * l1 * (1.0 / l2))).astype(BF16)
        if r_end > keys:
            a = jnp.concatenate([a, jnp.zeros((CHUNK, r_end - keys), BF16)], axis=1)
        return a, 1.0 / l1

    def values(r0, r1, a, inv_l1):
        o = jnp.dot(a, v_ref[0:r1, :], preferred_element_type=F32) * inv_l1
        y = o * lax.rsqrt(jnp.mean(o * o, axis=-1, keepdims=True) + EPS) * g
        o_ref[r0:r1, :] = (y * (1.0 - lam_init)).astype(BF16)

    nq = len(q_blocks)
    s_live, a_live = {}, {}
    for i in range(nq + 2):
        if i < nq:
            s_live[i] = scores(*q_blocks[i])
        if i >= 2:
            values(*q_blocks[i - 2], *a_live.pop(i - 2))
        if 1 <= i <= nq:
            a_live[i - 1] = weights(*q_blocks[i - 1], s_live.pop(i - 1))


def _attention(qk, v, lamp, subln_g, layer, B, L):
    T = B * L
    lam_init = 0.8 - 0.6 * math.exp(-0.3 * layer)
    q_blocks = ((0, CHUNK),) + tuple((r, r + TQ) for r in range(CHUNK, L, TQ))
    kh = A_QK // LANES
    return pl.pallas_call(
        functools.partial(_attn_kernel, lam_init, q_blocks),
        out_shape=jax.ShapeDtypeStruct((T, A_VW), BF16),
        grid=(B, A_HEADS),
        in_specs=[
            pl.BlockSpec((L, LANES), lambda b, h: (b, h)),
            pl.BlockSpec((L, LANES), lambda b, h: (b, kh + h)),
            pl.BlockSpec((L, A_V_DIM), lambda b, h: (b, h)),
            pl.BlockSpec((4, A_HEAD_DIM), lambda b, h: (0, 0)),
            pl.BlockSpec((1, A_V_DIM), lambda b, h: (0, 0)),
        ],
        out_specs=pl.BlockSpec((L, A_V_DIM), lambda b, h: (b, h)),
        compiler_params=_params("arbitrary", "arbitrary"),
        name="diff_attention",
    )(qk, qk, v, lamp, subln_g.reshape(1, A_V_DIM))


def _retention_kernel(nc, q_ref, k_ref, v_ref, rg_ref, gn_ref, decay_ref, tail_ref, qdec_ref,
                      cdec_ref, o_ref):
    decay = decay_ref[...]
    tail = tail_ref[...]
    qdec = qdec_ref[...]
    cdec = cdec_ref[...]
    gn = gn_ref[...]
    nt = (((1,), (1,)), ((), ()))
    tn = (((0,), (0,)), ((), ()))
    state = None
    for c in range(nc):
        rows = slice(c * CHUNK, (c + 1) * CHUNK)
        qc = q_ref[rows, :]
        kc = k_ref[rows, :]
        vc = v_ref[rows, :]
        s = lax.dot_general(qc.astype(BF16), kc.astype(BF16), nt, preferred_element_type=F32) * decay
        o = jnp.dot(s.astype(BF16), vc, preferred_element_type=F32)
        if state is not None:
            o = o + jnp.dot((qc * qdec).astype(BF16), state.astype(BF16), preferred_element_type=F32)
        if c + 1 < nc:
            kv = lax.dot_general((kc * tail).astype(BF16), vc, tn, preferred_element_type=F32)
            state = kv if state is None else cdec * state + kv
        mu = jnp.mean(o, axis=-1, keepdims=True)
        d = o - mu
        var = jnp.mean(d * d, axis=-1, keepdims=True)
        y = d * lax.rsqrt(var + EPS) * gn
        rg = rg_ref[rows, :]
        o_ref[rows, :] = (y * (rg * _sigmoid(rg))).astype(BF16)


def _retention(rqk, rv, gates, gn_g, tabs, B, L):
    T = B * L
    decay, tail, qdec, cdec = tabs
    kh = R_QK // R_QK_DIM
    return pl.pallas_call(
        functools.partial(_retention_kernel, L // CHUNK),
        out_shape=jax.ShapeDtypeStruct((T, R_VW), BF16),
        grid=(B, R_HEADS),
        in_specs=[
            pl.BlockSpec((L, R_QK_DIM), lambda b, h: (b, h)),
            pl.BlockSpec((L, R_QK_DIM), lambda b, h: (b, kh + h)),
            pl.BlockSpec((L, R_V_DIM), lambda b, h: (b, h)),
            pl.BlockSpec((L, R_V_DIM), lambda b, h: (b, h)),
            pl.BlockSpec((1, R_V_DIM), lambda b, h: (0, h)),
            pl.BlockSpec((None, CHUNK, CHUNK), lambda b, h: (h, 0, 0)),
            pl.BlockSpec((None, CHUNK, R_QK_DIM), lambda b, h: (h, 0, 0)),
            pl.BlockSpec((None, CHUNK, R_QK_DIM), lambda b, h: (h, 0, 0)),
            pl.BlockSpec((None, 1, R_V_DIM), lambda b, h: (h, 0, 0)),
        ],
        out_specs=pl.BlockSpec((L, R_V_DIM), lambda b, h: (b, h)),
        compiler_params=_params("arbitrary", "arbitrary"),
        name="retention",
    )(rqk, rqk, rv, gates, gn_g.reshape(1, R_VW), decay, tail, qdec, cdec)


def _merge_kernel(n_chunks, oa_ref, ob_ref, wa_ref, wb_ref, ga_ref, gb_ref, o_ref, was_ref, wbs_ref):
    s = pl.program_id(0)
    ck = wa_ref.shape[1]

    @pl.when(s < n_chunks)
    def _():
        col = pl.multiple_of(s * ck, ck)
        was_ref[:, pl.ds(col, ck)] = wa_ref[...].astype(BF16)

    @pl.when(jnp.logical_and(s >= n_chunks, s < 2 * n_chunks))
    def _():
        col = pl.multiple_of((s - n_chunks) * ck, ck)
        wbs_ref[:, pl.ds(col, ck)] = wb_ref[...].astype(BF16)

    @pl.when(s >= 2 * n_chunks)
    def _():
        def matmul(i):
            x_ref, ws_ref = ((oa_ref, was_ref), (ob_ref, wbs_ref))[i]
            return jnp.dot(x_ref[...], ws_ref[...], preferred_element_type=F32)

        gated = []

        def gate(i, branch):
            gated.append(_sigmoid((ga_ref, gb_ref)[i][...]) * branch)

        _software_pipeline(2, matmul, gate)
        o_ref[...] = (gated[0] + gated[1]).astype(BF16)


def _merge(oa, ob, w_a, w_b, gates, layer):
    T = oa.shape[0]
    tm, ck = TM_RESIDENT, STAGE_COLS
    nck = D_MODEL // ck
    row = lambda s: (jnp.maximum(s - 2 * nck, 0), 0)
    ga0 = (C_GA - C_RG) // D_MODEL
    gb0 = (C_GB - C_RG) // D_MODEL
    return pl.pallas_call(
        functools.partial(_merge_kernel, nck),
        out_shape=jax.ShapeDtypeStruct((T, D_MODEL), BF16),
        grid=(2 * nck + T // tm,),
        in_specs=[
            pl.BlockSpec((tm, A_VW), row),
            pl.BlockSpec((tm, R_VW), row),
            pl.BlockSpec((None, A_VW, ck), lambda s: (layer, 0, jnp.minimum(s, nck - 1))),
            pl.BlockSpec((None, R_VW, ck), lambda s: (layer, 0, jnp.clip(s - nck, 0, nck - 1))),
            pl.BlockSpec((tm, D_MODEL), lambda s: (jnp.maximum(s - 2 * nck, 0), ga0)),
            pl.BlockSpec((tm, D_MODEL), lambda s: (jnp.maximum(s - 2 * nck, 0), gb0)),
        ],
        out_specs=pl.BlockSpec((tm, D_MODEL), row),
        scratch_shapes=[pltpu.VMEM((A_VW, D_MODEL), BF16), pltpu.VMEM((R_VW, D_MODEL), BF16)],
        compiler_params=_params("arbitrary"),
        name="merge",
    )(oa, ob, w_a, w_b, gates, gates)


def _proj_ln_kernel(n_chunks, row_split, a_ref, w_ref, h_ref, g_ref, b_ref, of_ref, *rest):
    ob_ref, ws_ref = rest if len(rest) == 2 else (None, rest[0])
    s = pl.program_id(0)
    ck = w_ref.shape[1]

    @pl.when(s < n_chunks)
    def _():
        col = pl.multiple_of(s * ck, ck)
        ws_ref[:, pl.ds(col, ck)] = w_ref[...].astype(BF16)

    @pl.when(s >= n_chunks)
    def _():
        rb = a_ref.shape[0] // row_split

        def matmul(r):
            return jnp.dot(a_ref[r * rb:(r + 1) * rb, :], ws_ref[...], preferred_element_type=F32)

        def residual_ln(r, acc):
            rows = slice(r * rb, (r + 1) * rb)
            y = _layer_norm_rows(ALPHA * h_ref[rows, :] + acc, g_ref[...], b_ref[...])
            of_ref[rows, :] = y
            if ob_ref is not None:
                ob_ref[rows, :] = y.astype(BF16)

        _software_pipeline(row_split, matmul, residual_ln)


def _proj_ln(a, kdim, w, layer, hf, g, b, tm, ck, name):
    T = hf.shape[0]
    nck = D_MODEL // ck
    row = lambda s: (jnp.maximum(s - nck, 0), 0)
    return pl.pallas_call(
        functools.partial(_proj_ln_kernel, nck, tm // MIN_MATMUL_ROWS),
        out_shape=(jax.ShapeDtypeStruct((T, D_MODEL), F32), jax.ShapeDtypeStruct((T, D_MODEL), BF16)),
        grid=(nck + T // tm,),
        in_specs=[
            pl.BlockSpec((tm, kdim), row),
            pl.BlockSpec((None, kdim, ck), lambda s: (layer, 0, jnp.minimum(s, nck - 1))),
            pl.BlockSpec((tm, D_MODEL), row),
            pl.BlockSpec((None, 1, D_MODEL), lambda s: (layer, 0, 0)),
            pl.BlockSpec((None, 1, D_MODEL), lambda s: (layer, 0, 0)),
        ],
        out_specs=(pl.BlockSpec((tm, D_MODEL), row), pl.BlockSpec((tm, D_MODEL), row)),
        scratch_shapes=[pltpu.VMEM((kdim, D_MODEL), BF16)],
        compiler_params=_params("arbitrary"),
        name=name,
    )(a, w, hf, g.reshape(DEPTH, 1, D_MODEL), b.reshape(DEPTH, 1, D_MODEL))


def _proj_ln_final(a, kdim, w, layer, hf, g, b, tm, ck, nb, name):
    T = hf.shape[0]
    B = T // (nb * CHUNK)
    seq = (nb - 1) * CHUNK
    nck = D_MODEL // ck
    per_seq = seq // tm

    def rows(cols):
        def index(s):
            t = jnp.maximum(s - nck, 0)
            return ((t // per_seq) * nb + 1 + (t % per_seq) * (tm // CHUNK)) * CHUNK, 0
        return pl.BlockSpec((pl.Element(tm), pl.Element(cols)), index)

    return pl.pallas_call(
        functools.partial(_proj_ln_kernel, nck, 1),
        out_shape=jax.ShapeDtypeStruct((B * seq, D_MODEL), F32),
        grid=(nck + B * per_seq,),
        in_specs=[
            rows(kdim),
            pl.BlockSpec((None, kdim, ck), lambda s: (layer, 0, jnp.minimum(s, nck - 1))),
            rows(D_MODEL),
            pl.BlockSpec((None, 1, D_MODEL), lambda s: (layer, 0, 0)),
            pl.BlockSpec((None, 1, D_MODEL), lambda s: (layer, 0, 0)),
        ],
        out_specs=pl.BlockSpec((tm, D_MODEL), lambda s: (jnp.maximum(s - nck, 0), 0)),
        scratch_shapes=[pltpu.VMEM((kdim, D_MODEL), BF16)],
        compiler_params=_params("arbitrary"),
        name=name,
    )(a, w, hf, g.reshape(DEPTH, 1, D_MODEL), b.reshape(DEPTH, 1, D_MODEL))


def _ffn_up_kernel(rb, overlap, h_ref, wg_ref, wv_ref, cwg_ref, cwv_ref, cbg_ref, cbv_ref, o_ref,
                   wgs_ref, wvs_ref):
    tn = o_ref.shape[1]
    last_col_tile = pl.program_id(0) == pl.num_programs(0) - 1

    def placed(w):
        moved = jnp.concatenate([w[:, overlap:], jnp.zeros((w.shape[0], overlap), w.dtype)], axis=1)
        return jnp.where(last_col_tile, moved, w)

    @pl.when(pl.program_id(1) == 0)
    def _():
        wgs_ref[...] = placed(wg_ref[...].astype(BF16))
        wvs_ref[...] = placed(wv_ref[...].astype(BF16))

    row8 = lax.broadcasted_iota(jnp.int32, (HALO, tn), 0)
    halves = ((wgs_ref, cwg_ref, cbg_ref), (wvs_ref, cwv_ref, cbv_ref))
    prevs = [jnp.zeros((HALO, tn), F32)] * 2
    gates = []

    def matmul(i):
        r, half = divmod(i, 2)
        u = jnp.dot(h_ref[r * rb:(r + 1) * rb, :], halves[half][0][...], preferred_element_type=F32)
        if r == 0:
            u = jnp.concatenate([jnp.zeros((PAD, tn), F32), u[PAD:]], axis=0)
        return u

    def shifted(u, prev, k):
        rolled = pltpu.roll(u, k, 0)
        head = rolled[:HALO]
        for t in range(k):
            head = jnp.where(row8 == t, prev[HALO - k + t:HALO - k + t + 1], head)
        return jnp.concatenate([head, rolled[HALO:]], axis=0)

    def conv_geglu(i, u):
        r, half = divmod(i, 2)
        _, cw_ref, cb_ref = halves[half]
        prev = prevs[half]
        prevs[half] = u[rb - HALO:rb, :]
        scale = 1.0 if half == 0 else 0.5
        cw = placed(cw_ref[...]) * scale
        c = (shifted(u, prev, 2) * cw[0:1] + shifted(u, prev, 1) * cw[1:2] + u * cw[2:3]
             + placed(cb_ref[...]) * scale)
        if half == 0:
            gates.append(c)
        else:
            gate = gates.pop()
            o_ref[r * rb:(r + 1) * rb, :] = (gate * (1.0 + lax.erf(gate * (2.0 ** -0.5))) * c).astype(BF16)

    _software_pipeline(2 * (h_ref.shape[0] // rb), matmul, conv_geglu)


def _ffn_up(hb, w_up, conv_w, conv_b, layer, L):
    T = hb.shape[0]
    tm, tn = L, TN_FF
    assert PAD <= TM_BIG and L % TM_BIG == 0
    nt = pl.cdiv(D_FF, tn)
    overlap = nt * tn - D_FF
    lane_tile = lambda n: jnp.minimum(n * (tn // LANES), (D_FF - tn) // LANES)

    def cols(rows, half):
        return pl.BlockSpec((None, pl.Element(rows), pl.Element(tn)),
                            lambda n, m: (layer, 0, (lane_tile(n) + half * (D_FF // LANES)) * LANES))

    return pl.pallas_call(
        functools.partial(_ffn_up_kernel, TM_BIG, overlap),
        out_shape=jax.ShapeDtypeStruct((T, nt * tn), BF16),
        grid=(nt, T // tm),
        in_specs=[
            pl.BlockSpec((tm, D_MODEL), lambda n, m: (m, 0)),
            cols(D_MODEL, 0), cols(D_MODEL, 1),
            cols(CONV_W, 0), cols(CONV_W, 1),
            cols(1, 0), cols(1, 1),
        ],
        out_specs=pl.BlockSpec((tm, tn), lambda n, m: (m, n)),
        scratch_shapes=[pltpu.VMEM((D_MODEL, tn), BF16), pltpu.VMEM((D_MODEL, tn), BF16)],
        compiler_params=_params("arbitrary", "arbitrary"),
        name="ffn_up_conv_geglu",
    )(hb, w_up, w_up, conv_w, conv_w, conv_b, conv_b)


def _const(a):
    return jnp.asarray(np.ascontiguousarray(a, dtype=np.float32))


def _rotary_tables(L):
    pos = (np.arange(L) - PAD).astype(np.float64)
    a_freq = 1.0 / (ROPE_THETA ** (np.arange(0, A_HEAD_DIM, 2, dtype=np.float64) / A_HEAD_DIM))
    ang = pos[:, None] * a_freq[None, :]
    ang = np.concatenate([ang] * (LANES // (A_HEAD_DIM // 2)), axis=-1)
    lane = np.arange(LANES)
    sign_a = np.where((lane & (A_HEAD_DIM // 2)) == 0, -1.0, 1.0)
    cos_a, sin_a = np.cos(ang), np.sin(ang) * sign_a
    scale = (A_HEAD_DIM ** -0.5) * math.log2(math.e)
    cos_a = np.stack([cos_a * scale, cos_a])
    sin_a = np.stack([sin_a * scale, sin_a])

    r_freq = 1.0 / (ROPE_THETA ** np.linspace(0.0, 1.0, R_QK_DIM // 2, dtype=np.float64))
    ang = pos[:, None] * r_freq[None, :]
    ang = np.concatenate([ang, ang], axis=-1)
    sign_r = np.where(lane < R_QK_DIM // 2, -1.0, 1.0)
    cos_r = np.broadcast_to(np.cos(ang), (2, L, LANES))
    sin_r = np.broadcast_to(np.sin(ang) * sign_r, (2, L, LANES))
    valid = (np.arange(L) >= PAD).astype(np.float64)
    post_k = np.broadcast_to(((R_QK_DIM ** -0.5) * valid)[:, None], (L, LANES))
    post_r = np.stack([np.ones((L, LANES)), post_k])
    return (_const(cos_a), _const(sin_a)), (_const(cos_r), _const(sin_r), _const(post_r))


def _retention_tables():
    log_gamma = np.log(1.0 - 2.0 ** (-5.0 - np.arange(R_HEADS, dtype=np.float64)))
    p = np.arange(CHUNK, dtype=np.float64)
    diff = p[:, None] - p[None, :]
    lg = log_gamma[:, None, None]
    decay = np.where(diff >= 0, np.exp(lg * np.maximum(diff, 0.0)), 0.0)
    tail = np.exp(log_gamma[:, None] * (CHUNK - 1 - p))
    qdec = np.exp(log_gamma[:, None] * (p + 1.0))
    tail = np.broadcast_to(tail[:, :, None], (R_HEADS, CHUNK, R_QK_DIM))
    qdec = np.broadcast_to(qdec[:, :, None], (R_HEADS, CHUNK, R_QK_DIM))
    cdec = np.broadcast_to(np.exp(log_gamma * CHUNK)[:, None, None], (R_HEADS, 1, R_V_DIM))
    return _const(decay), _const(tail), _const(qdec), _const(cdec)


def kernel(x, meta_tokens, ln_emb_g, ln_emb_b, w_in, lam_q1, lam_k1, lam_q2, lam_k2, subln_g, ret_gn_g,
           w_branch_a, w_branch_b, w_out, ln1_g, ln1_b, w_up, conv_w, conv_b, w_down, ln2_g, ln2_b):
    B, seq, _ = x.shape
    nb = 1 + seq // CHUNK
    L = nb * CHUNK
    rot_a, rot_r = _rotary_tables(L)
    ret_tabs = _retention_tables()

    conv_b3 = conv_b.reshape(DEPTH, 1, 2 * D_FF)

    hf, hb = _embed_ln(x, meta_tokens, ln_emb_g, ln_emb_b, nb)
    for l in range(DEPTH):
        qk = _proj(hb, w_in, l, C_AQ, 2 * A_QK, "rot_a", BF16, rot_a, lambda n: n // (A_QK // TN_PROJ))
        av = _proj(hb, w_in, l, C_AV, A_VW, "plain", BF16)
        rqk = _proj(hb, w_in, l, C_RQ, 2 * R_QK, "rot_r", F32, rot_r, lambda n: n // (R_QK // TN_PROJ))
        rv = _proj(hb, w_in, l, C_RV, R_VW, "plain", BF16)
        gates = _proj(hb, w_in, l, C_RG, R_VW + 2 * D_MODEL, "plain", F32)

        lamp = jnp.stack([lam_q1[l], lam_k1[l], lam_q2[l], lam_k2[l]])
        oa = _attention(qk, av, lamp, subln_g[l], l, B, L)
        ob = _retention(rqk, rv, gates, ret_gn_g[l], ret_tabs, B, L)
        merged = _merge(oa, ob, w_branch_a, w_branch_b, gates, l)
        hf, hb = _proj_ln(merged, D_MODEL, w_out, l, hf, ln1_g, ln1_b, TM_SMALL, STAGE_COLS, "out_proj_ln")

        gact = _ffn_up(hb, w_up, conv_w, conv_b3, l, L)
        if l + 1 < DEPTH:
            hf, hb = _proj_ln(gact, D_FF, w_down, l, hf, ln2_g, ln2_b, TM_RESIDENT, STAGE_COLS_DOWN, "ffn_down_ln")
        else:
            out = _proj_ln_final(gact, D_FF, w_down, l, hf, ln2_g, ln2_b, TM_FINAL, STAGE_COLS_DOWN, nb,
                                 "ffn_down_ln_out")
    return out.reshape(B, seq, D_MODEL)
```

```python
import functools
import math

import jax
import jax.numpy as jnp
import numpy as np
from jax import lax
from jax.experimental import pallas as pl
from jax.experimental.pallas import tpu as pltpu

F32 = jnp.float32
BF16 = jnp.bfloat16

D_MODEL = 2048
DEPTH = 2
N_META = 16
CHUNK = 128
PAD = CHUNK - N_META
A_HEAD_DIM = 64
A_V_DIM = 2 * A_HEAD_DIM
A_HEADS = D_MODEL // A_V_DIM
A_QK = A_HEADS * 2 * A_HEAD_DIM
A_VW = A_HEADS * A_V_DIM
R_QK_DIM = 128
R_V_DIM = 2 * R_QK_DIM
R_HEADS = D_MODEL // R_V_DIM
R_QK = R_HEADS * R_QK_DIM
R_VW = R_HEADS * R_V_DIM
D_FF = 128 * ((8 * D_MODEL // 3 + 127) // 128)
CONV_W = 3
ROPE_THETA = 10000.0
ALPHA = (2 * DEPTH) ** 0.25
EPS = 1e-5
NEG = -1e30

LANES = 128
VMEM_LIMIT = 56 * 1024 * 1024

C_AQ = 0
C_AK = C_AQ + A_QK
C_AV = C_AK + A_QK
C_RQ = C_AV + A_VW
C_RK = C_RQ + R_QK
C_RV = C_RK + R_QK
C_RG = C_RV + R_VW
C_GA = C_RG + R_VW
C_GB = C_GA + D_MODEL
W_IN_COLS = C_GB + D_MODEL

TM_BIG = 1088
TM_SMALL = 544
TM_RESIDENT = 272
TM_FINAL = 2 * CHUNK
TN_PROJ = 1024
TN_FF = 512
STAGE_COLS = 512
STAGE_COLS_DOWN = 128
TQ = 256
RET_HEADS_PER_STEP = 2
ROW_SPLIT = 4
MIN_MATMUL_ROWS = 272
HALO = 8


def _params(*sem):
    return pltpu.CompilerParams(dimension_semantics=sem, vmem_limit_bytes=VMEM_LIMIT)


def _layer_norm_rows(z, g, b):
    mu = jnp.mean(z, axis=-1, keepdims=True)
    d = z - mu
    var = jnp.mean(d * d, axis=-1, keepdims=True)
    return d * lax.rsqrt(var + EPS) * g + b


def _sigmoid(x):
    return 1.0 / (1.0 + jnp.exp(-x))


def _software_pipeline(n, produce, consume):
    nxt = produce(0)
    for i in range(n):
        cur = nxt
        if i + 1 < n:
            nxt = produce(i + 1)
        consume(i, cur)


def _embed_ln_kernel(x_ref, meta_ref, g_ref, b_ref, hf_ref, hb_ref):
    g = g_ref[...]
    b = b_ref[...]
    tm = hf_ref.shape[0]

    def emit(rows):
        y = _layer_norm_rows(rows, g, b)
        hf_ref[...] = y
        hb_ref[...] = y.astype(BF16)

    @pl.when(pl.program_id(1) == 0)
    def _():
        emit(jnp.concatenate([jnp.zeros((PAD, D_MODEL), F32), meta_ref[...], x_ref[0:tm - CHUNK, :]], axis=0))

    @pl.when(pl.program_id(1) > 0)
    def _():
        emit(x_ref[...])


def _embed_ln(x, meta, g, b, nb):
    B, seq, _ = x.shape
    L = nb * CHUNK
    tm = TM_SMALL
    per_seq = L // tm
    step = 32
    assert tm % step == 0 and CHUNK % step == 0 and L % tm == 0

    def token_rows(bi, i):
        return (bi * (seq // step) + jnp.maximum(i * (tm // step) - CHUNK // step, 0)) * step, 0

    return pl.pallas_call(
        _embed_ln_kernel,
        out_shape=(jax.ShapeDtypeStruct((B * L, D_MODEL), F32), jax.ShapeDtypeStruct((B * L, D_MODEL), BF16)),
        grid=(B, per_seq),
        in_specs=[
            pl.BlockSpec((pl.Element(tm), pl.Element(D_MODEL)), token_rows),
            pl.BlockSpec((N_META, D_MODEL), lambda bi, i: (0, 0)),
            pl.BlockSpec((1, D_MODEL), lambda bi, i: (0, 0)),
            pl.BlockSpec((1, D_MODEL), lambda bi, i: (0, 0)),
        ],
        out_specs=(
            pl.BlockSpec((tm, D_MODEL), lambda bi, i: (bi * per_seq + i, 0)),
            pl.BlockSpec((tm, D_MODEL), lambda bi, i: (bi * per_seq + i, 0)),
        ),
        compiler_params=_params("arbitrary", "arbitrary"),
        name="embed_ln",
    )(x.reshape(B * seq, D_MODEL), meta, g.reshape(1, D_MODEL), b.reshape(1, D_MODEL))


def _proj_kernel(kind, h_ref, w_ref, *rest):
    if kind == "plain":
        o_ref, wb_ref = rest
    elif kind == "rot_a":
        cos_ref, sin_ref, o_ref, wb_ref = rest
    else:
        cos_ref, sin_ref, post_ref, o_ref, wb_ref = rest

    @pl.when(pl.program_id(1) == 0)
    def _():
        wb_ref[...] = w_ref[...].astype(BF16)

    tm, tn = o_ref.shape
    rb = tm // ROW_SPLIT
    if kind == "rot_a":
        lane = lax.broadcasted_iota(jnp.int32, (rb, LANES), 1)
        low_half = (lane & (A_HEAD_DIM // 2)) == 0

    def matmul(r):
        return jnp.dot(h_ref[r * rb:(r + 1) * rb, :], wb_ref[...], preferred_element_type=F32)

    def epilogue(r, acc):
        rows = slice(r * rb, (r + 1) * rb)
        if kind == "plain":
            o_ref[rows, :] = acc.astype(o_ref.dtype)
            return
        cos = cos_ref[rows, :]
        sin = sin_ref[rows, :]
        for j in range(tn // LANES):
            t = acc[:, j * LANES:(j + 1) * LANES]
            if kind == "rot_a":
                rot = jnp.where(low_half, pltpu.roll(t, LANES - A_HEAD_DIM // 2, 1),
                                pltpu.roll(t, A_HEAD_DIM // 2, 1))
                y = t * cos + rot * sin
            else:
                y = (t * cos + pltpu.roll(t, R_QK_DIM // 2, 1) * sin) * post_ref[rows, :]
            o_ref[rows, j * LANES:(j + 1) * LANES] = y.astype(o_ref.dtype)

    _software_pipeline(ROW_SPLIT, matmul, epilogue)


def _proj(hb, w_in, layer, col0, ncols, kind, out_dtype, tables=(), variant=None):
    T = hb.shape[0]
    tm, tn = TM_BIG, TN_PROJ
    nt, mt = ncols // tn, T // tm
    tile0 = col0 // tn
    in_specs = [
        pl.BlockSpec((tm, D_MODEL), lambda n, m: (m, 0)),
        pl.BlockSpec((None, D_MODEL, tn), lambda n, m: (layer, 0, tile0 + n)),
    ]
    for _ in tables:
        in_specs.append(pl.BlockSpec((None, tm, LANES), lambda n, m: (variant(n), m % 2, 0)))
    return pl.pallas_call(
        functools.partial(_proj_kernel, kind),
        out_shape=jax.ShapeDtypeStruct((T, ncols), out_dtype),
        grid=(nt, mt),
        in_specs=in_specs,
        out_specs=pl.BlockSpec((tm, tn), lambda n, m: (m, n)),
        scratch_shapes=[pltpu.VMEM((D_MODEL, tn), BF16)],
        compiler_params=_params("arbitrary", "arbitrary"),
        name="proj_" + kind,
    )(hb, w_in, *tables)


def _attn_kernel(lam_init, q_blocks, q_ref, k_ref, v_ref, lamp_ref, g_ref, o_ref):
    lp = lamp_ref[...]
    lam = (jnp.exp(jnp.sum(lp[0:1] * lp[1:2], axis=-1, keepdims=True))
           - jnp.exp(jnp.sum(lp[2:3] * lp[3:4], axis=-1, keepdims=True)) + lam_init)
    lane = lax.broadcasted_iota(jnp.int32, (1, LANES), 1)
    map1 = jnp.where(lane < A_HEAD_DIM, 1.0, 0.0).astype(BF16)
    map2 = jnp.where(lane >= A_HEAD_DIM, 1.0, 0.0).astype(BF16)
    g = g_ref[...]
    nt = (((1,), (1,)), ((), ()))

    def scores(r0, r1):
        q = q_ref[r0:r1, :]
        return lax.dot_general(jnp.concatenate([q * map1, q * map2], axis=0), k_ref[0:r1, :], nt,
                               preferred_element_type=F32)

    def weights(r0, r1, s12):
        tq = r1 - r0
        subs = [sub_weights(r0 + t, r1, s12[t:t + CHUNK], s12[tq + t:tq + t + CHUNK])
                for t in range(0, tq, CHUNK)]
        if len(subs) == 1:
            return subs[0]
        return tuple(jnp.concatenate(parts, axis=0) for parts in zip(*subs))

    def sub_weights(r0, r_end, s1, s2):
        keys = r0 + CHUNK
        qq = r0 + lax.broadcasted_iota(jnp.int32, (CHUNK, CHUNK), 0)
        kk = lax.broadcasted_iota(jnp.int32, (CHUNK, CHUNK), 1)

        def masked_probs(s):
            parts = []
            for c0 in range(0, keys, CHUNK):
                piece = s[:, c0:c0 + CHUNK]
                cond = None
                if c0 == 0:
                    cond = kk >= PAD
                    if r0 == 0:
                        cond = cond | (kk == qq)
                if c0 + CHUNK > r0:
                    causal = (kk + c0) <= qq
                    cond = causal if cond is None else (cond & causal)
                if cond is not None:
                    piece = jnp.where(cond, piece, NEG)
                parts.append(piece)
            s = parts[0] if len(parts) == 1 else jnp.concatenate(parts, axis=1)
            p = jnp.exp2(s - jnp.max(s, axis=-1, keepdims=True))
            return p, jnp.sum(p, axis=-1, keepdims=True)

        p1, l1 = masked_probs(s1)
        p2, l2 = masked_probs(s2)
        a = (p1 - p2 * (lam * l1 * (1.0 / l2))).astype(BF16)
        if r_end > keys:
            a = jnp.concatenate([a, jnp.zeros((CHUNK, r_end - keys), BF16)], axis=1)
        return a, 1.0 / l1

    def values(r0, r1, a, inv_l1):
        o = jnp.dot(a, v_ref[0:r1, :], preferred_element_type=F32) * inv_l1
        y = o * lax.rsqrt(jnp.mean(o * o, axis=-1, keepdims=True) + EPS) * g
        o_ref[r0:r1, :] = (y * (1.0 - lam_init)).astype(BF16)

    nq = len(q_blocks)
    s_live, a_live = {}, {}
    for i in range(nq + 2):
        if i < nq:
            s_live[i] = scores(*q_blocks[i])
        if i >= 2:
            values(*q_blocks[i - 2], *a_live.pop(i - 2))
        if 1 <= i <= nq:
            a_live[i - 1] = weights(*q_blocks[i - 1], s_live.pop(i - 1))


def _attention(qk, v, lamp, subln_g, layer, B, L):
    T = B * L
    lam_init = 0.8 - 0.6 * math.exp(-0.3 * layer)
    q_blocks = ((0, CHUNK),) + tuple((r, r + TQ) for r in range(CHUNK, L, TQ))
    kh = A_QK // LANES
    return pl.pallas_call(
        functools.partial(_attn_kernel, lam_init, q_blocks),
        out_shape=jax.ShapeDtypeStruct((T, A_VW), BF16),
        grid=(B, A_HEADS),
        in_specs=[
            pl.BlockSpec((L, LANES), lambda b, h: (b, h)),
            pl.BlockSpec((L, LANES), lambda b, h: (b, kh + h)),
            pl.BlockSpec((L, A_V_DIM), lambda b, h: (b, h)),
            pl.BlockSpec((4, A_HEAD_DIM), lambda b, h: (0, 0)),
            pl.BlockSpec((1, A_V_DIM), lambda b, h: (0, 0)),
        ],
        out_specs=pl.BlockSpec((L, A_V_DIM), lambda b, h: (b, h)),
        compiler_params=_params("arbitrary", "arbitrary"),
        name="diff_attention",
    )(qk, qk, v, lamp, subln_g.reshape(1, A_V_DIM))


def _retention_kernel(nc, q_ref, k_ref, v_ref, rg_ref, gn_ref, decay_ref, tail_ref, qdec_ref,
                      cdec_ref, o_ref):
    nt = (((1,), (1,)), ((), ()))
    tn = (((0,), (0,)), ((), ()))
    states = [None] * RET_HEADS_PER_STEP
    for c in range(nc):
        rows = slice(c * CHUNK, (c + 1) * CHUNK)
        for j in range(RET_HEADS_PER_STEP):
            qk_cols = slice(j * R_QK_DIM, (j + 1) * R_QK_DIM)
            v_cols = slice(j * R_V_DIM, (j + 1) * R_V_DIM)
            qc = q_ref[rows, qk_cols]
            kc = k_ref[rows, qk_cols]
            vc = v_ref[rows, v_cols]
            s = lax.dot_general(qc.astype(BF16), kc.astype(BF16), nt, preferred_element_type=F32) * decay_ref[j]
            o = jnp.dot(s.astype(BF16), vc, preferred_element_type=F32)
            if states[j] is not None:
                o = o + jnp.dot((qc * qdec_ref[j]).astype(BF16), states[j].astype(BF16),
                                preferred_element_type=F32)
            if c + 1 < nc:
                kv = lax.dot_general((kc * tail_ref[j]).astype(BF16), vc, tn, preferred_element_type=F32)
                states[j] = kv if states[j] is None else cdec_ref[j] * states[j] + kv
            mu = jnp.mean(o, axis=-1, keepdims=True)
            d = o - mu
            var = jnp.mean(d * d, axis=-1, keepdims=True)
            y = d * lax.rsqrt(var + EPS) * gn_ref[:, v_cols]
            rg = rg_ref[rows, v_cols]
            o_ref[rows, v_cols] = (y * (rg * _sigmoid(rg))).astype(BF16)


def _retention(rqk, rv, gates, gn_g, tabs, B, L):
    T = B * L
    decay, tail, qdec, cdec = tabs
    hp = RET_HEADS_PER_STEP
    kh = R_QK // (hp * R_QK_DIM)
    return pl.pallas_call(
        functools.partial(_retention_kernel, L // CHUNK),
        out_shape=jax.ShapeDtypeStruct((T, R_VW), BF16),
        grid=(B, R_HEADS // hp),
        in_specs=[
            pl.BlockSpec((L, hp * R_QK_DIM), lambda b, h: (b, h)),
            pl.BlockSpec((L, hp * R_QK_DIM), lambda b, h: (b, kh + h)),
            pl.BlockSpec((L, hp * R_V_DIM), lambda b, h: (b, h)),
            pl.BlockSpec((L, hp * R_V_DIM), lambda b, h: (b, h)),
            pl.BlockSpec((1, hp * R_V_DIM), lambda b, h: (0, h)),
            pl.BlockSpec((hp, CHUNK, CHUNK), lambda b, h: (h, 0, 0)),
            pl.BlockSpec((hp, CHUNK, R_QK_DIM), lambda b, h: (h, 0, 0)),
            pl.BlockSpec((hp, CHUNK, R_QK_DIM), lambda b, h: (h, 0, 0)),
            pl.BlockSpec((hp, 1, R_V_DIM), lambda b, h: (h, 0, 0)),
        ],
        out_specs=pl.BlockSpec((L, hp * R_V_DIM), lambda b, h: (b, h)),
        compiler_params=_params("arbitrary", "arbitrary"),
        name="retention",
    )(rqk, rqk, rv, gates, gn_g.reshape(1, R_VW), decay, tail, qdec, cdec)


def _merge_kernel(n_chunks, oa_ref, ob_ref, wa_ref, wb_ref, ga_ref, gb_ref, o_ref, was_ref, wbs_ref):
    s = pl.program_id(0)
    ck = wa_ref.shape[1]

    @pl.when(s < n_chunks)
    def _():
        col = pl.multiple_of(s * ck, ck)
        was_ref[:, pl.ds(col, ck)] = wa_ref[...].astype(BF16)

    @pl.when(jnp.logical_and(s >= n_chunks, s < 2 * n_chunks))
    def _():
        col = pl.multiple_of((s - n_chunks) * ck, ck)
        wbs_ref[:, pl.ds(col, ck)] = wb_ref[...].astype(BF16)

    @pl.when(s >= 2 * n_chunks)
    def _():
        def matmul(i):
            x_ref, ws_ref = ((oa_ref, was_ref), (ob_ref, wbs_ref))[i]
            return jnp.dot(x_ref[...], ws_ref[...], preferred_element_type=F32)

        gated = []

        def gate(i, branch):
            gated.append(_sigmoid((ga_ref, gb_ref)[i][...]) * branch)

        _software_pipeline(2, matmul, gate)
        o_ref[...] = (gated[0] + gated[1]).astype(BF16)


def _merge(oa, ob, w_a, w_b, gates, layer):
    T = oa.shape[0]
    tm, ck = TM_RESIDENT, STAGE_COLS
    nck = D_MODEL // ck
    row = lambda s: (jnp.maximum(s - 2 * nck, 0), 0)
    ga0 = (C_GA - C_RG) // D_MODEL
    gb0 = (C_GB - C_RG) // D_MODEL
    return pl.pallas_call(
        functools.partial(_merge_kernel, nck),
        out_shape=jax.ShapeDtypeStruct((T, D_MODEL), BF16),
        grid=(2 * nck + T // tm,),
        in_specs=[
            pl.BlockSpec((tm, A_VW), row),
            pl.BlockSpec((tm, R_VW), row),
            pl.BlockSpec((None, A_VW, ck), lambda s: (layer, 0, jnp.minimum(s, nck - 1))),
            pl.BlockSpec((None, R_VW, ck), lambda s: (layer, 0, jnp.clip(s - nck, 0, nck - 1))),
            pl.BlockSpec((tm, D_MODEL), lambda s: (jnp.maximum(s - 2 * nck, 0), ga0)),
            pl.BlockSpec((tm, D_MODEL), lambda s: (jnp.maximum(s - 2 * nck, 0), gb0)),
        ],
        out_specs=pl.BlockSpec((tm, D_MODEL), row),
        scratch_shapes=[pltpu.VMEM((A_VW, D_MODEL), BF16), pltpu.VMEM((R_VW, D_MODEL), BF16)],
        compiler_params=_params("arbitrary"),
        name="merge",
    )(oa, ob, w_a, w_b, gates, gates)


def _proj_ln_kernel(n_chunks, row_split, a_ref, w_ref, h_ref, g_ref, b_ref, of_ref, *rest):
    ob_ref, ws_ref = rest if len(rest) == 2 else (None, rest[0])
    s = pl.program_id(0)
    ck = w_ref.shape[1]

    @pl.when(s < n_chunks)
    def _():
        col = pl.multiple_of(s * ck, ck)
        ws_ref[:, pl.ds(col, ck)] = w_ref[...].astype(BF16)

    @pl.when(s >= n_chunks)
    def _():
        rb = a_ref.shape[0] // row_split

        def matmul(r):
            return jnp.dot(a_ref[r * rb:(r + 1) * rb, :], ws_ref[...], preferred_element_type=F32)

        def residual_ln(r, acc):
            rows = slice(r * rb, (r + 1) * rb)
            y = _layer_norm_rows(ALPHA * h_ref[rows, :] + acc, g_ref[...], b_ref[...])
            of_ref[rows, :] = y
            if ob_ref is not None:
                ob_ref[rows, :] = y.astype(BF16)

        _software_pipeline(row_split, matmul, residual_ln)


def _proj_ln(a, kdim, w, layer, hf, g, b, tm, ck, name):
    T = hf.shape[0]
    nck = D_MODEL // ck
    row = lambda s: (jnp.maximum(s - nck, 0), 0)
    return pl.pallas_call(
        functools.partial(_proj_ln_kernel, nck, tm // MIN_MATMUL_ROWS),
        out_shape=(jax.ShapeDtypeStruct((T, D_MODEL), F32), jax.ShapeDtypeStruct((T, D_MODEL), BF16)),
        grid=(nck + T // tm,),
        in_specs=[
            pl.BlockSpec((tm, kdim), row),
            pl.BlockSpec((None, kdim, ck), lambda s: (layer, 0, jnp.minimum(s, nck - 1))),
            pl.BlockSpec((tm, D_MODEL), row),
            pl.BlockSpec((None, 1, D_MODEL), lambda s: (layer, 0, 0)),
            pl.BlockSpec((None, 1, D_MODEL), lambda s: (layer, 0, 0)),
        ],
        out_specs=(pl.BlockSpec((tm, D_MODEL), row), pl.BlockSpec((tm, D_MODEL), row)),
        scratch_shapes=[pltpu.VMEM((kdim, D_MODEL), BF16)],
        compiler_params=_params("arbitrary"),
        name=name,
    )(a, w, hf, g.reshape(DEPTH, 1, D_MODEL), b.reshape(DEPTH, 1, D_MODEL))


def _proj_ln_final(a, kdim, w, layer, hf, g, b, tm, ck, nb, name):
    T = hf.shape[0]
    B = T // (nb * CHUNK)
    seq = (nb - 1) * CHUNK
    nck = D_MODEL // ck
    per_seq = seq // tm

    def rows(cols):
        def index(s):
            t = jnp.maximum(s - nck, 0)
            return ((t // per_seq) * nb + 1 + (t % per_seq) * (tm // CHUNK)) * CHUNK, 0
        return pl.BlockSpec((pl.Element(tm), pl.Element(cols)), index)

    return pl.pallas_call(
        functools.partial(_proj_ln_kernel, nck, 1),
        out_shape=jax.ShapeDtypeStruct((B * seq, D_MODEL), F32),
        grid=(nck + B * per_seq,),
        in_specs=[
            rows(kdim),
            pl.BlockSpec((None, kdim, ck), lambda s: (layer, 0, jnp.minimum(s, nck - 1))),
            rows(D_MODEL),
            pl.BlockSpec((None, 1, D_MODEL), lambda s: (layer, 0, 0)),
            pl.BlockSpec((None, 1, D_MODEL), lambda s: (layer, 0, 0)),
        ],
        out_specs=pl.BlockSpec((tm, D_MODEL), lambda s: (jnp.maximum(s - nck, 0), 0)),
        scratch_shapes=[pltpu.VMEM((kdim, D_MODEL), BF16)],
        compiler_params=_params("arbitrary"),
        name=name,
    )(a, w, hf, g.reshape(DEPTH, 1, D_MODEL), b.reshape(DEPTH, 1, D_MODEL))


def _ffn_up_kernel(rb, overlap, h_ref, wg_ref, wv_ref, cwg_ref, cwv_ref, cbg_ref, cbv_ref, o_ref,
                   wgs_ref, wvs_ref):
    tn = o_ref.shape[1]
    last_col_tile = pl.program_id(0) == pl.num_programs(0) - 1

    def placed(w):
        moved = jnp.concatenate([w[:, overlap:], jnp.zeros((w.shape[0], overlap), w.dtype)], axis=1)
        return jnp.where(last_col_tile, moved, w)

    @pl.when(pl.program_id(1) == 0)
    def _():
        wgs_ref[...] = placed(wg_ref[...].astype(BF16))
        wvs_ref[...] = placed(wv_ref[...].astype(BF16))

    row8 = lax.broadcasted_iota(jnp.int32, (HALO, tn), 0)
    halves = ((wgs_ref, cwg_ref, cbg_ref), (wvs_ref, cwv_ref, cbv_ref))
    prevs = [jnp.zeros((HALO, tn), F32)] * 2
    gates = []

    def matmul(i):
        r, half = divmod(i, 2)
        u = jnp.dot(h_ref[r * rb:(r + 1) * rb, :], halves[half][0][...], preferred_element_type=F32)
        if r == 0:
            u = jnp.concatenate([jnp.zeros((PAD, tn), F32), u[PAD:]], axis=0)
        return u

    def shifted(u, prev, k):
        rolled = pltpu.roll(u, k, 0)
        head = rolled[:HALO]
        for t in range(k):
            head = jnp.where(row8 == t, prev[HALO - k + t:HALO - k + t + 1], head)
        return jnp.concatenate([head, rolled[HALO:]], axis=0)

    def conv_geglu(i, u):
        r, half = divmod(i, 2)
        _, cw_ref, cb_ref = halves[half]
        prev = prevs[half]
        prevs[half] = u[rb - HALO:rb, :]
        scale = 1.0 if half == 0 else 0.5
        cw = placed(cw_ref[...]) * scale
        c = (shifted(u, prev, 2) * cw[0:1] + shifted(u, prev, 1) * cw[1:2] + u * cw[2:3]
             + placed(cb_ref[...]) * scale)
        if half == 0:
            gates.append(c)
        else:
            gate = gates.pop()
            o_ref[r * rb:(r + 1) * rb, :] = (gate * (1.0 + lax.erf(gate * (2.0 ** -0.5))) * c).astype(BF16)

    _software_pipeline(2 * (h_ref.shape[0] // rb), matmul, conv_geglu)


def _ffn_up(hb, w_up, conv_w, conv_b, layer, L):
    T = hb.shape[0]
    tm, tn = L, TN_FF
    assert PAD <= TM_BIG and L % TM_BIG == 0
    nt = pl.cdiv(D_FF, tn)
    overlap = nt * tn - D_FF
    lane_tile = lambda n: jnp.minimum(n * (tn // LANES), (D_FF - tn) // LANES)

    def cols(rows, half):
        return pl.BlockSpec((None, pl.Element(rows), pl.Element(tn)),
                            lambda n, m: (layer, 0, (lane_tile(n) + half * (D_FF // LANES)) * LANES))

    return pl.pallas_call(
        functools.partial(_ffn_up_kernel, TM_BIG, overlap),
        out_shape=jax.ShapeDtypeStruct((T, nt * tn), BF16),
        grid=(nt, T // tm),
        in_specs=[
            pl.BlockSpec((tm, D_MODEL), lambda n, m: (m, 0)),
            cols(D_MODEL, 0), cols(D_MODEL, 1),
            cols(CONV_W, 0), cols(CONV_W, 1),
            cols(1, 0), cols(1, 1),
        ],
        out_specs=pl.BlockSpec((tm, tn), lambda n, m: (m, n)),
        scratch_shapes=[pltpu.VMEM((D_MODEL, tn), BF16), pltpu.VMEM((D_MODEL, tn), BF16)],
        compiler_params=_params("arbitrary", "arbitrary"),
        name="ffn_up_conv_geglu",
    )(hb, w_up, w_up, conv_w, conv_w, conv_b, conv_b)


def _const(a):
    return jnp.asarray(np.ascontiguousarray(a, dtype=np.float32))


def _rotary_tables(L):
    pos = (np.arange(L) - PAD).astype(np.float64)
    a_freq = 1.0 / (ROPE_THETA ** (np.arange(0, A_HEAD_DIM, 2, dtype=np.float64) / A_HEAD_DIM))
    ang = pos[:, None] * a_freq[None, :]
    ang = np.concatenate([ang] * (LANES // (A_HEAD_DIM // 2)), axis=-1)
    lane = np.arange(LANES)
    sign_a = np.where((lane & (A_HEAD_DIM // 2)) == 0, -1.0, 1.0)
    cos_a, sin_a = np.cos(ang), np.sin(ang) * sign_a
    scale = (A_HEAD_DIM ** -0.5) * math.log2(math.e)
    cos_a = np.stack([cos_a * scale, cos_a])
    sin_a = np.stack([sin_a * scale, sin_a])

    r_freq = 1.0 / (ROPE_THETA ** np.linspace(0.0, 1.0, R_QK_DIM // 2, dtype=np.float64))
    ang = pos[:, None] * r_freq[None, :]
    ang = np.concatenate([ang, ang], axis=-1)
    sign_r = np.where(lane < R_QK_DIM // 2, -1.0, 1.0)
    cos_r = np.broadcast_to(np.cos(ang), (2, L, LANES))
    sin_r = np.broadcast_to(np.sin(ang) * sign_r, (2, L, LANES))
    valid = (np.arange(L) >= PAD).astype(np.float64)
    post_k = np.broadcast_to(((R_QK_DIM ** -0.5) * valid)[:, None], (L, LANES))
    post_r = np.stack([np.ones((L, LANES)), post_k])
    return (_const(cos_a), _const(sin_a)), (_const(cos_r), _const(sin_r), _const(post_r))


def _retention_tables():
    log_gamma = np.log(1.0 - 2.0 ** (-5.0 - np.arange(R_HEADS, dtype=np.float64)))
    p = np.arange(CHUNK, dtype=np.float64)
    diff = p[:, None] - p[None, :]
    lg = log_gamma[:, None, None]
    decay = np.where(diff >= 0, np.exp(lg * np.maximum(diff, 0.0)), 0.0)
    tail = np.exp(log_gamma[:, None] * (CHUNK - 1 - p))
    qdec = np.exp(log_gamma[:, None] * (p + 1.0))
    tail = np.broadcast_to(tail[:, :, None], (R_HEADS, CHUNK, R_QK_DIM))
    qdec = np.broadcast_to(qdec[:, :, None], (R_HEADS, CHUNK, R_QK_DIM))
    cdec = np.broadcast_to(np.exp(log_gamma * CHUNK)[:, None, None], (R_HEADS, 1, R_V_DIM))
    return _const(decay), _const(tail), _const(qdec), _const(cdec)


def kernel(x, meta_tokens, ln_emb_g, ln_emb_b, w_in, lam_q1, lam_k1, lam_q2, lam_k2, subln_g, ret_gn_g,
           w_branch_a, w_branch_b, w_out, ln1_g, ln1_b, w_up, conv_w, conv_b, w_down, ln2_g, ln2_b):
    B, seq, _ = x.shape
    nb = 1 + seq // CHUNK
    L = nb * CHUNK
    rot_a, rot_r = _rotary_tables(L)
    ret_tabs = _retention_tables()

    conv_b3 = conv_b.reshape(DEPTH, 1, 2 * D_FF)

    hf, hb = _embed_ln(x, meta_tokens, ln_emb_g, ln_emb_b, nb)
    for l in range(DEPTH):
        qk = _proj(hb, w_in, l, C_AQ, 2 * A_QK, "rot_a", BF16, rot_a, lambda n: n // (A_QK // TN_PROJ))
        av = _proj(hb, w_in, l, C_AV, A_VW, "plain", BF16)
        rqk = _proj(hb, w_in, l, C_RQ, 2 * R_QK, "rot_r", F32, rot_r, lambda n: n // (R_QK // TN_PROJ))
        rv = _proj(hb, w_in, l, C_RV, R_VW, "plain", BF16)
        gates = _proj(hb, w_in, l, C_RG, R_VW + 2 * D_MODEL, "plain", F32)

        lamp = jnp.stack([lam_q1[l], lam_k1[l], lam_q2[l], lam_k2[l]])
        oa = _attention(qk, av, lamp, subln_g[l], l, B, L)
        ob = _retention(rqk, rv, gates, ret_gn_g[l], ret_tabs, B, L)
        merged = _merge(oa, ob, w_branch_a, w_branch_b, gates, l)
        hf, hb = _proj_ln(merged, D_MODEL, w_out, l, hf, ln1_g, ln1_b, TM_SMALL, STAGE_COLS, "out_proj_ln")

        gact = _ffn_up(hb, w_up, conv_w, conv_b3, l, L)
        if l + 1 < DEPTH:
            hf, hb = _proj_ln(gact, D_FF, w_down, l, hf, ln2_g, ln2_b, TM_RESIDENT, STAGE_COLS_DOWN, "ffn_down_ln")
        else:
            out = _proj_ln_final(gact, D_FF, w_down, l, hf, ln2_g, ln2_b, TM_FINAL, STAGE_COLS_DOWN, nb,
                                 "ffn_down_ln_out")
    return out.reshape(B, seq, D_MODEL)
```

```python
import functools
import math

import jax
import jax.numpy as jnp
import numpy as np
from jax import lax
from jax.experimental import pallas as pl
from jax.experimental.pallas import tpu as pltpu

F32 = jnp.float32
BF16 = jnp.bfloat16

D_MODEL = 2048
DEPTH = 2
N_META = 16
CHUNK = 128
PAD = CHUNK - N_META
A_HEAD_DIM = 64
A_V_DIM = 2 * A_HEAD_DIM
A_HEADS = D_MODEL // A_V_DIM
A_QK = A_HEADS * 2 * A_HEAD_DIM
A_VW = A_HEADS * A_V_DIM
R_QK_DIM = 128
R_V_DIM = 2 * R_QK_DIM
R_HEADS = D_MODEL // R_V_DIM
R_QK = R_HEADS * R_QK_DIM
R_VW = R_HEADS * R_V_DIM
D_FF = 128 * ((8 * D_MODEL // 3 + 127) // 128)
CONV_W = 3
ROPE_THETA = 10000.0
ALPHA = (2 * DEPTH) ** 0.25
EPS = 1e-5
NEG = -1e30

LANES = 128
VMEM_LIMIT = 56 * 1024 * 1024

C_AQ = 0
C_AK = C_AQ + A_QK
C_AV = C_AK + A_QK
C_RQ = C_AV + A_VW
C_RK = C_RQ + R_QK
C_RV = C_RK + R_QK
C_RG = C_RV + R_VW
C_GA = C_RG + R_VW
C_GB = C_GA + D_MODEL
W_IN_COLS = C_GB + D_MODEL

TM_BIG = 1088
TM_SMALL = 544
TM_RESIDENT = 272
TM_FINAL = 2 * CHUNK
TN_PROJ = 1024
TN_FF = 512
STAGE_COLS = 512
STAGE_COLS_DOWN = 256
TQ = 256
RET_HEADS_PER_STEP = 2
ROW_SPLIT = 4
MIN_MATMUL_ROWS = 272
HALO = 8


def _params(*sem):
    return pltpu.CompilerParams(dimension_semantics=sem, vmem_limit_bytes=VMEM_LIMIT)


def _layer_norm_rows(z, g, b):
    mu = jnp.mean(z, axis=-1, keepdims=True)
    d = z - mu
    var = jnp.mean(d * d, axis=-1, keepdims=True)
    return d * lax.rsqrt(var + EPS) * g + b


def _sigmoid(x):
    return 1.0 / (1.0 + jnp.exp(-x))


def _software_pipeline(n, produce, consume):
    nxt = produce(0)
    for i in range(n):
        cur = nxt
        if i + 1 < n:
            nxt = produce(i + 1)
        consume(i, cur)


def _embed_ln_kernel(x_ref, meta_ref, g_ref, b_ref, hf_ref, hb_ref):
    g = g_ref[...]
    b = b_ref[...]
    tm = hf_ref.shape[0]

    def emit(rows):
        y = _layer_norm_rows(rows, g, b)
        hf_ref[...] = y
        hb_ref[...] = y.astype(BF16)

    @pl.when(pl.program_id(1) == 0)
    def _():
        emit(jnp.concatenate([jnp.zeros((PAD, D_MODEL), F32), meta_ref[...], x_ref[0:tm - CHUNK, :]], axis=0))

    @pl.when(pl.program_id(1) > 0)
    def _():
        emit(x_ref[...])


def _embed_ln(x, meta, g, b, nb):
    B, seq, _ = x.shape
    L = nb * CHUNK
    tm = TM_BIG
    per_seq = L // tm
    step = 32
    assert tm % step == 0 and CHUNK % step == 0 and L % tm == 0

    def token_rows(bi, i):
        return (bi * (seq // step) + jnp.maximum(i * (tm // step) - CHUNK // step, 0)) * step, 0

    return pl.pallas_call(
        _embed_ln_kernel,
        out_shape=(jax.ShapeDtypeStruct((B * L, D_MODEL), F32), jax.ShapeDtypeStruct((B * L, D_MODEL), BF16)),
        grid=(B, per_seq),
        in_specs=[
            pl.BlockSpec((pl.Element(tm), pl.Element(D_MODEL)), token_rows),
            pl.BlockSpec((N_META, D_MODEL), lambda bi, i: (0, 0)),
            pl.BlockSpec((1, D_MODEL), lambda bi, i: (0, 0)),
            pl.BlockSpec((1, D_MODEL), lambda bi, i: (0, 0)),
        ],
        out_specs=(
            pl.BlockSpec((tm, D_MODEL), lambda bi, i: (bi * per_seq + i, 0)),
            pl.BlockSpec((tm, D_MODEL), lambda bi, i: (bi * per_seq + i, 0)),
        ),
        compiler_params=_params("arbitrary", "arbitrary"),
        name="embed_ln",
    )(x.reshape(B * seq, D_MODEL), meta, g.reshape(1, D_MODEL), b.reshape(1, D_MODEL))


def _proj_kernel(kind, h_ref, w_ref, *rest):
    if kind == "plain":
        o_ref, wb_ref = rest
    elif kind == "rot_a":
        cos_ref, sin_ref, o_ref, wb_ref = rest
    else:
        cos_ref, sin_ref, post_ref, o_ref, wb_ref = rest

    @pl.when(pl.program_id(1) == 0)
    def _():
        wb_ref[...] = w_ref[...].astype(BF16)

    tm, tn = o_ref.shape
    rb = tm // ROW_SPLIT
    if kind == "rot_a":
        lane = lax.broadcasted_iota(jnp.int32, (rb, LANES), 1)
        low_half = (lane & (A_HEAD_DIM // 2)) == 0

    def matmul(r):
        return jnp.dot(h_ref[r * rb:(r + 1) * rb, :], wb_ref[...], preferred_element_type=F32)

    def epilogue(r, acc):
        rows = slice(r * rb, (r + 1) * rb)
        if kind == "plain":
            o_ref[rows, :] = acc.astype(o_ref.dtype)
            return
        cos = cos_ref[rows, :]
        sin = sin_ref[rows, :]
        for j in range(tn // LANES):
            t = acc[:, j * LANES:(j + 1) * LANES]
            if kind == "rot_a":
                rot = jnp.where(low_half, pltpu.roll(t, LANES - A_HEAD_DIM // 2, 1),
                                pltpu.roll(t, A_HEAD_DIM // 2, 1))
                y = t * cos + rot * sin
            else:
                y = (t * cos + pltpu.roll(t, R_QK_DIM // 2, 1) * sin) * post_ref[rows, :]
            o_ref[rows, j * LANES:(j + 1) * LANES] = y.astype(o_ref.dtype)

    _software_pipeline(ROW_SPLIT, matmul, epilogue)


def _proj(hb, w_in, layer, col0, ncols, kind, out_dtype, tables=(), variant=None):
    T = hb.shape[0]
    tm, tn = TM_BIG, TN_PROJ
    nt, mt = ncols // tn, T // tm
    tile0 = col0 // tn
    in_specs = [
        pl.BlockSpec((tm, D_MODEL), lambda n, m: (m, 0)),
        pl.BlockSpec((None, D_MODEL, tn), lambda n, m: (layer, 0, tile0 + n)),
    ]
    for _ in tables:
        in_specs.append(pl.BlockSpec((None, tm, LANES), lambda n, m: (variant(n), m % 2, 0)))
    return pl.pallas_call(
        functools.partial(_proj_kernel, kind),
        out_shape=jax.ShapeDtypeStruct((T, ncols), out_dtype),
        grid=(nt, mt),
        in_specs=in_specs,
        out_specs=pl.BlockSpec((tm, tn), lambda n, m: (m, n)),
        scratch_shapes=[pltpu.VMEM((D_MODEL, tn), BF16)],
        compiler_params=_params("arbitrary", "arbitrary"),
        name="proj_" + kind,
    )(hb, w_in, *tables)


def _attn_kernel(lam_init, q_blocks, q_ref, k_ref, v_ref, lamp_ref, g_ref, o_ref):
    lp = lamp_ref[...]
    lam = (jnp.exp(jnp.sum(lp[0:1] * lp[1:2], axis=-1, keepdims=True))
           - jnp.exp(jnp.sum(lp[2:3] * lp[3:4], axis=-1, keepdims=True)) + lam_init)
    lane = lax.broadcasted_iota(jnp.int32, (1, LANES), 1)
    map1 = jnp.where(lane < A_HEAD_DIM, 1.0, 0.0).astype(BF16)
    map2 = jnp.where(lane >= A_HEAD_DIM, 1.0, 0.0).astype(BF16)
    g = g_ref[...]
    nt = (((1,), (1,)), ((), ()))

    def scores(r0, r1):
        q = q_ref[r0:r1, :]
        return lax.dot_general(jnp.concatenate([q * map1, q * map2], axis=0), k_ref[0:r1, :], nt,
                               preferred_element_type=F32)

    def weights(r0, r1, s12):
        tq = r1 - r0
        subs = [sub_weights(r0 + t, r1, s12[t:t + CHUNK], s12[tq + t:tq + t + CHUNK])
                for t in range(0, tq, CHUNK)]
        if len(subs) == 1:
            return subs[0]
        return tuple(jnp.concatenate(parts, axis=0) for parts in zip(*subs))

    def sub_weights(r0, r_end, s1, s2):
        keys = r0 + CHUNK
        qq = r0 + lax.broadcasted_iota(jnp.int32, (CHUNK, CHUNK), 0)
        kk = lax.broadcasted_iota(jnp.int32, (CHUNK, CHUNK), 1)

        def masked_probs(s):
            parts = []
            for c0 in range(0, keys, CHUNK):
                piece = s[:, c0:c0 + CHUNK]
                cond = None
                if c0 == 0:
                    cond = kk >= PAD
                    if r0 == 0:
                        cond = cond | (kk == qq)
                if c0 + CHUNK > r0:
                    causal = (kk + c0) <= qq
                    cond = causal if cond is None else (cond & causal)
                if cond is not None:
                    piece = jnp.where(cond, piece, NEG)
                parts.append(piece)
            s = parts[0] if len(parts) == 1 else jnp.concatenate(parts, axis=1)
            p = jnp.exp2(s - jnp.max(s, axis=-1, keepdims=True))
            return p, jnp.sum(p, axis=-1, keepdims=True)

        p1, l1 = masked_probs(s1)
        p2, l2 = masked_probs(s2)
        a = (p1 - p2 * (lam * l1 * (1.0 / l2))).astype(BF16)
        if r_end > keys:
            a = jnp.concatenate([a, jnp.zeros((CHUNK, r_end - keys), BF16)], axis=1)
        return a, 1.0 / l1

    def values(r0, r1, a, inv_l1):
        o = jnp.dot(a, v_ref[0:r1, :], preferred_element_type=F32) * inv_l1
        y = o * lax.rsqrt(jnp.mean(o * o, axis=-1, keepdims=True) + EPS) * g
        o_ref[r0:r1, :] = (y * (1.0 - lam_init)).astype(BF16)

    nq = len(q_blocks)
    s_live, a_live = {}, {}
    for i in range(nq + 2):
        if i < nq:
            s_live[i] = scores(*q_blocks[i])
        if i >= 2:
            values(*q_blocks[i - 2], *a_live.pop(i - 2))
        if 1 <= i <= nq:
            a_live[i - 1] = weights(*q_blocks[i - 1], s_live.pop(i - 1))


def _attention(qk, v, lamp, subln_g, layer, B, L):
    T = B * L
    lam_init = 0.8 - 0.6 * math.exp(-0.3 * layer)
    q_blocks = ((0, CHUNK),) + tuple((r, r + TQ) for r in range(CHUNK, L, TQ))
    kh = A_QK // LANES
    return pl.pallas_call(
        functools.partial(_attn_kernel, lam_init, q_blocks),
        out_shape=jax.ShapeDtypeStruct((T, A_VW), BF16),
        grid=(B, A_HEADS),
        in_specs=[
            pl.BlockSpec((L, LANES), lambda b, h: (b, h)),
            pl.BlockSpec((L, LANES), lambda b, h: (b, kh + h)),
            pl.BlockSpec((L, A_V_DIM), lambda b, h: (b, h)),
            pl.BlockSpec((4, A_HEAD_DIM), lambda b, h: (0, 0)),
            pl.BlockSpec((1, A_V_DIM), lambda b, h: (0, 0)),
        ],
        out_specs=pl.BlockSpec((L, A_V_DIM), lambda b, h: (b, h)),
        compiler_params=_params("arbitrary", "arbitrary"),
        name="diff_attention",
    )(qk, qk, v, lamp, subln_g.reshape(1, A_V_DIM))


def _retention_kernel(nc, q_ref, k_ref, v_ref, rg_ref, gn_ref, decay_ref, tail_ref, qdec_ref,
                      cdec_ref, o_ref):
    nt = (((1,), (1,)), ((), ()))
    tn = (((0,), (0,)), ((), ()))
    states = [None] * RET_HEADS_PER_STEP
    for c in range(nc):
        rows = slice(c * CHUNK, (c + 1) * CHUNK)
        for j in range(RET_HEADS_PER_STEP):
            qk_cols = slice(j * R_QK_DIM, (j + 1) * R_QK_DIM)
            v_cols = slice(j * R_V_DIM, (j + 1) * R_V_DIM)
            qc = q_ref[rows, qk_cols]
            kc = k_ref[rows, qk_cols]
            vc = v_ref[rows, v_cols]
            s = lax.dot_general(qc.astype(BF16), kc.astype(BF16), nt, preferred_element_type=F32) * decay_ref[j]
            o = jnp.dot(s.astype(BF16), vc, preferred_element_type=F32)
            if states[j] is not None:
                o = o + jnp.dot((qc * qdec_ref[j]).astype(BF16), states[j].astype(BF16),
                                preferred_element_type=F32)
            if c + 1 < nc:
                kv = lax.dot_general((kc * tail_ref[j]).astype(BF16), vc, tn, preferred_element_type=F32)
                states[j] = kv if states[j] is None else cdec_ref[j] * states[j] + kv
            mu = jnp.mean(o, axis=-1, keepdims=True)
            d = o - mu
            var = jnp.mean(d * d, axis=-1, keepdims=True)
            y = d * lax.rsqrt(var + EPS) * gn_ref[:, v_cols]
            rg = rg_ref[rows, v_cols]
            o_ref[rows, v_cols] = (y * (rg * _sigmoid(rg))).astype(BF16)


def _retention(rqk, rv, gates, gn_g, tabs, B, L):
    T = B * L
    decay, tail, qdec, cdec = tabs
    hp = RET_HEADS_PER_STEP
    kh = R_QK // (hp * R_QK_DIM)
    return pl.pallas_call(
        functools.partial(_retention_kernel, L // CHUNK),
        out_shape=jax.ShapeDtypeStruct((T, R_VW), BF16),
        grid=(B, R_HEADS // hp),
        in_specs=[
            pl.BlockSpec((L, hp * R_QK_DIM), lambda b, h: (b, h)),
            pl.BlockSpec((L, hp * R_QK_DIM), lambda b, h: (b, kh + h)),
            pl.BlockSpec((L, hp * R_V_DIM), lambda b, h: (b, h)),
            pl.BlockSpec((L, hp * R_V_DIM), lambda b, h: (b, h)),
            pl.BlockSpec((1, hp * R_V_DIM), lambda b, h: (0, h)),
            pl.BlockSpec((hp, CHUNK, CHUNK), lambda b, h: (h, 0, 0)),
            pl.BlockSpec((hp, CHUNK, R_QK_DIM), lambda b, h: (h, 0, 0)),
            pl.BlockSpec((hp, CHUNK, R_QK_DIM), lambda b, h: (h, 0, 0)),
            pl.BlockSpec((hp, 1, R_V_DIM), lambda b, h: (h, 0, 0)),
        ],
        out_specs=pl.BlockSpec((L, hp * R_V_DIM), lambda b, h: (b, h)),
        compiler_params=_params("arbitrary", "arbitrary"),
        name="retention",
    )(rqk, rqk, rv, gates, gn_g.reshape(1, R_VW), decay, tail, qdec, cdec)


def _merge_kernel(n_chunks, oa_ref, ob_ref, wa_ref, wb_ref, ga_ref, gb_ref, o_ref, was_ref, wbs_ref):
    s = pl.program_id(0)
    ck = wa_ref.shape[1]

    @pl.when(s < n_chunks)
    def _():
        col = pl.multiple_of(s * ck, ck)
        was_ref[:, pl.ds(col, ck)] = wa_ref[...].astype(BF16)

    @pl.when(jnp.logical_and(s >= n_chunks, s < 2 * n_chunks))
    def _():
        col = pl.multiple_of((s - n_chunks) * ck, ck)
        wbs_ref[:, pl.ds(col, ck)] = wb_ref[...].astype(BF16)

    @pl.when(s >= 2 * n_chunks)
    def _():
        def matmul(i):
            x_ref, ws_ref = ((oa_ref, was_ref), (ob_ref, wbs_ref))[i]
            return jnp.dot(x_ref[...], ws_ref[...], preferred_element_type=F32)

        gated = []

        def gate(i, branch):
            gated.append(_sigmoid((ga_ref, gb_ref)[i][...]) * branch)

        _software_pipeline(2, matmul, gate)
        o_ref[...] = (gated[0] + gated[1]).astype(BF16)


def _merge(oa, ob, w_a, w_b, gates, layer):
    T = oa.shape[0]
    tm, ck = TM_RESIDENT, STAGE_COLS
    nck = D_MODEL // ck
    row = lambda s: (jnp.maximum(s - 2 * nck, 0), 0)
    ga0 = (C_GA - C_RG) // D_MODEL
    gb0 = (C_GB - C_RG) // D_MODEL
    return pl.pallas_call(
        functools.partial(_merge_kernel, nck),
        out_shape=jax.ShapeDtypeStruct((T, D_MODEL), BF16),
        grid=(2 * nck + T // tm,),
        in_specs=[
            pl.BlockSpec((tm, A_VW), row),
            pl.BlockSpec((tm, R_VW), row),
            pl.BlockSpec((None, A_VW, ck), lambda s: (layer, 0, jnp.minimum(s, nck - 1))),
            pl.BlockSpec((None, R_VW, ck), lambda s: (layer, 0, jnp.clip(s - nck, 0, nck - 1))),
            pl.BlockSpec((tm, D_MODEL), lambda s: (jnp.maximum(s - 2 * nck, 0), ga0)),
            pl.BlockSpec((tm, D_MODEL), lambda s: (jnp.maximum(s - 2 * nck, 0), gb0)),
        ],
        out_specs=pl.BlockSpec((tm, D_MODEL), row),
        scratch_shapes=[pltpu.VMEM((A_VW, D_MODEL), BF16), pltpu.VMEM((R_VW, D_MODEL), BF16)],
        compiler_params=_params("arbitrary"),
        name="merge",
    )(oa, ob, w_a, w_b, gates, gates)


def _proj_ln_kernel(n_chunks, row_split, a_ref, w_ref, h_ref, g_ref, b_ref, of_ref, *rest):
    ob_ref, ws_ref = rest if len(rest) == 2 else (None, rest[0])
    s = pl.program_id(0)
    ck = w_ref.shape[1]

    @pl.when(s < n_chunks)
    def _():
        col = pl.multiple_of(s * ck, ck)
        ws_ref[:, pl.ds(col, ck)] = w_ref[...].astype(BF16)

    @pl.when(s >= n_chunks)
    def _():
        rb = a_ref.shape[0] // row_split

        def matmul(r):
            return jnp.dot(a_ref[r * rb:(r + 1) * rb, :], ws_ref[...], preferred_element_type=F32)

        def residual_ln(r, acc):
            rows = slice(r * rb, (r + 1) * rb)
            y = _layer_norm_rows(ALPHA * h_ref[rows, :] + acc, g_ref[...], b_ref[...])
            of_ref[rows, :] = y
            if ob_ref is not None:
                ob_ref[rows, :] = y.astype(BF16)

        _software_pipeline(row_split, matmul, residual_ln)


def _proj_ln(a, kdim, w, layer, hf, g, b, tm, ck, name):
    T = hf.shape[0]
    nck = D_MODEL // ck
    row = lambda s: (jnp.maximum(s - nck, 0), 0)
    return pl.pallas_call(
        functools.partial(_proj_ln_kernel, nck, tm // MIN_MATMUL_ROWS),
        out_shape=(jax.ShapeDtypeStruct((T, D_MODEL), F32), jax.ShapeDtypeStruct((T, D_MODEL), BF16)),
        grid=(nck + T // tm,),
        in_specs=[
            pl.BlockSpec((tm, kdim), row),
            pl.BlockSpec((None, kdim, ck), lambda s: (layer, 0, jnp.minimum(s, nck - 1))),
            pl.BlockSpec((tm, D_MODEL), row),
            pl.BlockSpec((None, 1, D_MODEL), lambda s: (layer, 0, 0)),
            pl.BlockSpec((None, 1, D_MODEL), lambda s: (layer, 0, 0)),
        ],
        out_specs=(pl.BlockSpec((tm, D_MODEL), row), pl.BlockSpec((tm, D_MODEL), row)),
        scratch_shapes=[pltpu.VMEM((kdim, D_MODEL), BF16)],
        compiler_params=_params("arbitrary"),
        name=name,
    )(a, w, hf, g.reshape(DEPTH, 1, D_MODEL), b.reshape(DEPTH, 1, D_MODEL))


def _proj_ln_final(a, kdim, w, layer, hf, g, b, tm, ck, nb, name):
    T = hf.shape[0]
    B = T // (nb * CHUNK)
    seq = (nb - 1) * CHUNK
    nck = D_MODEL // ck
    per_seq = seq // tm

    def rows(cols):
        def index(s):
            t = jnp.maximum(s - nck, 0)
            return ((t // per_seq) * nb + 1 + (t % per_seq) * (tm // CHUNK)) * CHUNK, 0
        return pl.BlockSpec((pl.Element(tm), pl.Element(cols)), index)

    return pl.pallas_call(
        functools.partial(_proj_ln_kernel, nck, 1),
        out_shape=jax.ShapeDtypeStruct((B * seq, D_MODEL), F32),
        grid=(nck + B * per_seq,),
        in_specs=[
            rows(kdim),
            pl.BlockSpec((None, kdim, ck), lambda s: (layer, 0, jnp.minimum(s, nck - 1))),
            rows(D_MODEL),
            pl.BlockSpec((None, 1, D_MODEL), lambda s: (layer, 0, 0)),
            pl.BlockSpec((None, 1, D_MODEL), lambda s: (layer, 0, 0)),
        ],
        out_specs=pl.BlockSpec((tm, D_MODEL), lambda s: (jnp.maximum(s - nck, 0), 0)),
        scratch_shapes=[pltpu.VMEM((kdim, D_MODEL), BF16)],
        compiler_params=_params("arbitrary"),
        name=name,
    )(a, w, hf, g.reshape(DEPTH, 1, D_MODEL), b.reshape(DEPTH, 1, D_MODEL))


def _ffn_up_kernel(rb, overlap, h_ref, wg_ref, wv_ref, cwg_ref, cwv_ref, cbg_ref, cbv_ref, o_ref,
                   wgs_ref, wvs_ref):
    tn = o_ref.shape[1]
    last_col_tile = pl.program_id(0) == pl.num_programs(0) - 1

    def placed(w):
        moved = jnp.concatenate([w[:, overlap:], jnp.zeros((w.shape[0], overlap), w.dtype)], axis=1)
        return jnp.where(last_col_tile, moved, w)

    @pl.when(pl.program_id(1) == 0)
    def _():
        wgs_ref[...] = placed(wg_ref[...].astype(BF16))
        wvs_ref[...] = placed(wv_ref[...].astype(BF16))

    row8 = lax.broadcasted_iota(jnp.int32, (HALO, tn), 0)
    halves = ((wgs_ref, cwg_ref, cbg_ref), (wvs_ref, cwv_ref, cbv_ref))
    prevs = [jnp.zeros((HALO, tn), F32)] * 2
    gates = []

    def matmul(i):
        r, half = divmod(i, 2)
        u = jnp.dot(h_ref[r * rb:(r + 1) * rb, :], halves[half][0][...], preferred_element_type=F32)
        if r == 0:
            u = jnp.concatenate([jnp.zeros((PAD, tn), F32), u[PAD:]], axis=0)
        return u

    def shifted(u, prev, k):
        rolled = pltpu.roll(u, k, 0)
        head = rolled[:HALO]
        for t in range(k):
            head = jnp.where(row8 == t, prev[HALO - k + t:HALO - k + t + 1], head)
        return jnp.concatenate([head, rolled[HALO:]], axis=0)

    def conv_geglu(i, u):
        r, half = divmod(i, 2)
        _, cw_ref, cb_ref = halves[half]
        prev = prevs[half]
        prevs[half] = u[rb - HALO:rb, :]
        scale = 1.0 if half == 0 else 0.5
        cw = placed(cw_ref[...]) * scale
        c = (shifted(u, prev, 2) * cw[0:1] + shifted(u, prev, 1) * cw[1:2] + u * cw[2:3]
             + placed(cb_ref[...]) * scale)
        if half == 0:
            gates.append(c)
        else:
            gate = gates.pop()
            o_ref[r * rb:(r + 1) * rb, :] = (gate * (1.0 + lax.erf(gate * (2.0 ** -0.5))) * c).astype(BF16)

    _software_pipeline(2 * (h_ref.shape[0] // rb), matmul, conv_geglu)


def _ffn_up(hb, w_up, conv_w, conv_b, layer, L):
    T = hb.shape[0]
    tm, tn = L, TN_FF
    assert PAD <= TM_BIG and L % TM_BIG == 0
    nt = pl.cdiv(D_FF, tn)
    overlap = nt * tn - D_FF
    lane_tile = lambda n: jnp.minimum(n * (tn // LANES), (D_FF - tn) // LANES)

    def cols(rows, half):
        return pl.BlockSpec((None, pl.Element(rows), pl.Element(tn)),
                            lambda n, m: (layer, 0, (lane_tile(n) + half * (D_FF // LANES)) * LANES))

    return pl.pallas_call(
        functools.partial(_ffn_up_kernel, TM_BIG, overlap),
        out_shape=jax.ShapeDtypeStruct((T, nt * tn), BF16),
        grid=(nt, T // tm),
        in_specs=[
            pl.BlockSpec((tm, D_MODEL), lambda n, m: (m, 0)),
            cols(D_MODEL, 0), cols(D_MODEL, 1),
            cols(CONV_W, 0), cols(CONV_W, 1),
            cols(1, 0), cols(1, 1),
        ],
        out_specs=pl.BlockSpec((tm, tn), lambda n, m: (m, n)),
        scratch_shapes=[pltpu.VMEM((D_MODEL, tn), BF16), pltpu.VMEM((D_MODEL, tn), BF16)],
        compiler_params=_params("arbitrary", "arbitrary"),
        name="ffn_up_conv_geglu",
    )(hb, w_up, w_up, conv_w, conv_w, conv_b, conv_b)


def _const(a):
    return jnp.asarray(np.ascontiguousarray(a, dtype=np.float32))


def _rotary_tables(L):
    pos = (np.arange(L) - PAD).astype(np.float64)
    a_freq = 1.0 / (ROPE_THETA ** (np.arange(0, A_HEAD_DIM, 2, dtype=np.float64) / A_HEAD_DIM))
    ang = pos[:, None] * a_freq[None, :]
    ang = np.concatenate([ang] * (LANES // (A_HEAD_DIM // 2)), axis=-1)
    lane = np.arange(LANES)
    sign_a = np.where((lane & (A_HEAD_DIM // 2)) == 0, -1.0, 1.0)
    cos_a, sin_a = np.cos(ang), np.sin(ang) * sign_a
    scale = (A_HEAD_DIM ** -0.5) * math.log2(math.e)
    cos_a = np.stack([cos_a * scale, cos_a])
    sin_a = np.stack([sin_a * scale, sin_a])

    r_freq = 1.0 / (ROPE_THETA ** np.linspace(0.0, 1.0, R_QK_DIM // 2, dtype=np.float64))
    ang = pos[:, None] * r_freq[None, :]
    ang = np.concatenate([ang, ang], axis=-1)
    sign_r = np.where(lane < R_QK_DIM // 2, -1.0, 1.0)
    cos_r = np.broadcast_to(np.cos(ang), (2, L, LANES))
    sin_r = np.broadcast_to(np.sin(ang) * sign_r, (2, L, LANES))
    valid = (np.arange(L) >= PAD).astype(np.float64)
    post_k = np.broadcast_to(((R_QK_DIM ** -0.5) * valid)[:, None], (L, LANES))
    post_r = np.stack([np.ones((L, LANES)), post_k])
    return (_const(cos_a), _const(sin_a)), (_const(cos_r), _const(sin_r), _const(post_r))


def _retention_tables():
    log_gamma = np.log(1.0 - 2.0 ** (-5.0 - np.arange(R_HEADS, dtype=np.float64)))
    p = np.arange(CHUNK, dtype=np.float64)
    diff = p[:, None] - p[None, :]
    lg = log_gamma[:, None, None]
    decay = np.where(diff >= 0, np.exp(lg * np.maximum(diff, 0.0)), 0.0)
    tail = np.exp(log_gamma[:, None] * (CHUNK - 1 - p))
    qdec = np.exp(log_gamma[:, None] * (p + 1.0))
    tail = np.broadcast_to(tail[:, :, None], (R_HEADS, CHUNK, R_QK_DIM))
    qdec = np.broadcast_to(qdec[:, :, None], (R_HEADS, CHUNK, R_QK_DIM))
    cdec = np.broadcast_to(np.exp(log_gamma * CHUNK)[:, None, None], (R_HEADS, 1, R_V_DIM))
    return _const(decay), _const(tail), _const(qdec), _const(cdec)


def kernel(x, meta_tokens, ln_emb_g, ln_emb_b, w_in, lam_q1, lam_k1, lam_q2, lam_k2, subln_g, ret_gn_g,
           w_branch_a, w_branch_b, w_out, ln1_g, ln1_b, w_up, conv_w, conv_b, w_down, ln2_g, ln2_b):
    B, seq, _ = x.shape
    assert seq % CHUNK == 0 and x.shape[2] == D_MODEL and meta_tokens.shape == (N_META, D_MODEL)
    assert w_in.shape == (DEPTH, D_MODEL, W_IN_COLS) and w_up.shape == (DEPTH, D_MODEL, 2 * D_FF)
    nb = 1 + seq // CHUNK
    L = nb * CHUNK
    rot_a, rot_r = _rotary_tables(L)
    ret_tabs = _retention_tables()

    conv_b3 = conv_b.reshape(DEPTH, 1, 2 * D_FF)

    hf, hb = _embed_ln(x, meta_tokens, ln_emb_g, ln_emb_b, nb)
    for l in range(DEPTH):
        qk = _proj(hb, w_in, l, C_AQ, 2 * A_QK, "rot_a", BF16, rot_a, lambda n: n // (A_QK // TN_PROJ))
        av = _proj(hb, w_in, l, C_AV, A_VW, "plain", BF16)
        rqk = _proj(hb, w_in, l, C_RQ, 2 * R_QK, "rot_r", F32, rot_r, lambda n: n // (R_QK // TN_PROJ))
        rv = _proj(hb, w_in, l, C_RV, R_VW, "plain", BF16)
        gates = _proj(hb, w_in, l, C_RG, R_VW + 2 * D_MODEL, "plain", F32)

        lamp = jnp.stack([lam_q1[l], lam_k1[l], lam_q2[l], lam_k2[l]])
        oa = _attention(qk, av, lamp, subln_g[l], l, B, L)
        ob = _retention(rqk, rv, gates, ret_gn_g[l], ret_tabs, B, L)
        merged = _merge(oa, ob, w_branch_a, w_branch_b, gates, l)
        hf, hb = _proj_ln(merged, D_MODEL, w_out, l, hf, ln1_g, ln1_b, TM_SMALL, STAGE_COLS, "out_proj_ln")

        gact = _ffn_up(hb, w_up, conv_w, conv_b3, l, L)
        if l + 1 < DEPTH:
            hf, hb = _proj_ln(gact, D_FF, w_down, l, hf, ln2_g, ln2_b, TM_RESIDENT, STAGE_COLS_DOWN, "ffn_down_ln")
        else:
            out = _proj_ln_final(gact, D_FF, w_down, l, hf, ln2_g, ln2_b, TM_FINAL, STAGE_COLS_DOWN, nb,
                                 "ffn_down_ln_out")
    return out.reshape(B, seq, D_MODEL)
```

```python
import functools
import math

import jax
import jax.numpy as jnp
import numpy as np
from jax import lax
from jax.experimental import pallas as pl
from jax.experimental.pallas import tpu as pltpu

F32 = jnp.float32
BF16 = jnp.bfloat16

D_MODEL = 2048
DEPTH = 2
N_META = 16
CHUNK = 128
PAD = CHUNK - N_META
A_HEAD_DIM = 64
A_V_DIM = 2 * A_HEAD_DIM
A_HEADS = D_MODEL // A_V_DIM
A_QK = A_HEADS * 2 * A_HEAD_DIM
A_VW = A_HEADS * A_V_DIM
R_QK_DIM = 128
R_V_DIM = 2 * R_QK_DIM
R_HEADS = D_MODEL // R_V_DIM
R_QK = R_HEADS * R_QK_DIM
R_VW = R_HEADS * R_V_DIM
D_FF = 128 * ((8 * D_MODEL // 3 + 127) // 128)
CONV_W = 3
ROPE_THETA = 10000.0
ALPHA = (2 * DEPTH) ** 0.25
EPS = 1e-5
NEG = -1e30

LANES = 128
VMEM_LIMIT = 56 * 1024 * 1024

C_AQ = 0
C_AK = C_AQ + A_QK
C_AV = C_AK + A_QK
C_RQ = C_AV + A_VW
C_RK = C_RQ + R_QK
C_RV = C_RK + R_QK
C_RG = C_RV + R_VW
C_GA = C_RG + R_VW
C_GB = C_GA + D_MODEL
W_IN_COLS = C_GB + D_MODEL

TM_BIG = 1088
TM_SMALL = 544
TM_RESIDENT = 272
TM_FINAL = 2 * CHUNK
TN_PROJ = 1024
TN_FF = 512
STAGE_COLS = 512
STAGE_COLS_DOWN = 256
TQ = 256
RET_HEADS_PER_STEP = 2
ROW_SPLIT = 4
MIN_MATMUL_ROWS = 272
HALO = 8


def _params(*sem):
    return pltpu.CompilerParams(dimension_semantics=sem, vmem_limit_bytes=VMEM_LIMIT)


def _layer_norm_rows(z, g, b):
    mu = jnp.mean(z, axis=-1, keepdims=True)
    d = z - mu
    var = jnp.mean(d * d, axis=-1, keepdims=True)
    return d * lax.rsqrt(var + EPS) * g + b


def _sigmoid(x):
    return 1.0 / (1.0 + jnp.exp(-x))


def _software_pipeline(n, produce, consume):
    nxt = produce(0)
    for i in range(n):
        cur = nxt
        if i + 1 < n:
            nxt = produce(i + 1)
        consume(i, cur)


def _embed_ln_kernel(x_ref, meta_ref, g_ref, b_ref, hf_ref, hb_ref):
    g = g_ref[...]
    b = b_ref[...]
    tm = hf_ref.shape[0]

    def emit(rows):
        y = _layer_norm_rows(rows, g, b)
        hf_ref[...] = y
        hb_ref[...] = y.astype(BF16)

    @pl.when(pl.program_id(1) == 0)
    def _():
        emit(jnp.concatenate([jnp.zeros((PAD, D_MODEL), F32), meta_ref[...], x_ref[0:tm - CHUNK, :]], axis=0))

    @pl.when(pl.program_id(1) > 0)
    def _():
        emit(x_ref[...])


def _embed_ln(x, meta, g, b, nb):
    B, seq, _ = x.shape
    L = nb * CHUNK
    tm = TM_BIG
    per_seq = L // tm
    step = 32
    assert tm % step == 0 and CHUNK % step == 0 and L % tm == 0

    def token_rows(bi, i):
        return (bi * (seq // step) + jnp.maximum(i * (tm // step) - CHUNK // step, 0)) * step, 0

    return pl.pallas_call(
        _embed_ln_kernel,
        out_shape=(jax.ShapeDtypeStruct((B * L, D_MODEL), F32), jax.ShapeDtypeStruct((B * L, D_MODEL), BF16)),
        grid=(B, per_seq),
        in_specs=[
            pl.BlockSpec((pl.Element(tm), pl.Element(D_MODEL)), token_rows),
            pl.BlockSpec((N_META, D_MODEL), lambda bi, i: (0, 0)),
            pl.BlockSpec((1, D_MODEL), lambda bi, i: (0, 0)),
            pl.BlockSpec((1, D_MODEL), lambda bi, i: (0, 0)),
        ],
        out_specs=(
            pl.BlockSpec((tm, D_MODEL), lambda bi, i: (bi * per_seq + i, 0)),
            pl.BlockSpec((tm, D_MODEL), lambda bi, i: (bi * per_seq + i, 0)),
        ),
        compiler_params=_params("arbitrary", "arbitrary"),
        name="embed_ln",
    )(x.reshape(B * seq, D_MODEL), meta, g.reshape(1, D_MODEL), b.reshape(1, D_MODEL))


def _proj_kernel(kind, h_ref, w_ref, *rest):
    if kind == "plain":
        o_ref, wb_ref = rest
    elif kind == "rot_a":
        cos_ref, sin_ref, o_ref, wb_ref = rest
    else:
        cos_ref, sin_ref, post_ref, o_ref, wb_ref = rest

    @pl.when(pl.program_id(1) == 0)
    def _():
        wb_ref[...] = w_ref[...].astype(BF16)

    tm, tn = o_ref.shape
    rb = tm // ROW_SPLIT
    if kind == "rot_a":
        lane = lax.broadcasted_iota(jnp.int32, (rb, LANES), 1)
        low_half = (lane & (A_HEAD_DIM // 2)) == 0

    def matmul(r):
        return jnp.dot(h_ref[r * rb:(r + 1) * rb, :], wb_ref[...], preferred_element_type=F32)

    def epilogue(r, acc):
        rows = slice(r * rb, (r + 1) * rb)
        if kind == "plain":
            o_ref[rows, :] = acc.astype(o_ref.dtype)
            return
        cos = cos_ref[rows, :]
        sin = sin_ref[rows, :]
        for j in range(tn // LANES):
            t = acc[:, j * LANES:(j + 1) * LANES]
            if kind == "rot_a":
                rot = jnp.where(low_half, pltpu.roll(t, LANES - A_HEAD_DIM // 2, 1),
                                pltpu.roll(t, A_HEAD_DIM // 2, 1))
                y = t * cos + rot * sin
            else:
                y = (t * cos + pltpu.roll(t, R_QK_DIM // 2, 1) * sin) * post_ref[rows, :]
            o_ref[rows, j * LANES:(j + 1) * LANES] = y.astype(o_ref.dtype)

    _software_pipeline(ROW_SPLIT, matmul, epilogue)


def _proj(hb, w_in, layer, col0, ncols, kind, out_dtype, tables=(), variant=None):
    T = hb.shape[0]
    tm, tn = TM_BIG, TN_PROJ
    nt, mt = ncols // tn, T // tm
    tile0 = col0 // tn
    in_specs = [
        pl.BlockSpec((tm, D_MODEL), lambda n, m: (m, 0)),
        pl.BlockSpec((None, D_MODEL, tn), lambda n, m: (layer, 0, tile0 + n)),
    ]
    for _ in tables:
        in_specs.append(pl.BlockSpec((None, tm, LANES), lambda n, m: (variant(n), m % 2, 0)))
    return pl.pallas_call(
        functools.partial(_proj_kernel, kind),
        out_shape=jax.ShapeDtypeStruct((T, ncols), out_dtype),
        grid=(nt, mt),
        in_specs=in_specs,
        out_specs=pl.BlockSpec((tm, tn), lambda n, m: (m, n)),
        scratch_shapes=[pltpu.VMEM((D_MODEL, tn), BF16)],
        compiler_params=_params("arbitrary", "arbitrary"),
        name="proj_" + kind,
    )(hb, w_in, *tables)


def _attn_kernel(lam_init, q_blocks, q_ref, k_ref, v_ref, lamp_ref, g_ref, o_ref):
    lp = lamp_ref[...]
    lam = (jnp.exp(jnp.sum(lp[0:1] * lp[1:2], axis=-1, keepdims=True))
           - jnp.exp(jnp.sum(lp[2:3] * lp[3:4], axis=-1, keepdims=True)) + lam_init)
    lane = lax.broadcasted_iota(jnp.int32, (1, LANES), 1)
    map1 = jnp.where(lane < A_HEAD_DIM, 1.0, 0.0).astype(BF16)
    map2 = jnp.where(lane >= A_HEAD_DIM, 1.0, 0.0).astype(BF16)
    g = g_ref[...]
    nt = (((1,), (1,)), ((), ()))

    def scores(r0, r1):
        q = q_ref[r0:r1, :]
        return lax.dot_general(jnp.concatenate([q * map1, q * map2], axis=0), k_ref[0:r1, :], nt,
                               preferred_element_type=F32)

    def weights(r0, r1, s12):
        tq = r1 - r0
        subs = [sub_weights(r0 + t, r1, s12[t:t + CHUNK], s12[tq + t:tq + t + CHUNK])
                for t in range(0, tq, CHUNK)]
        if len(subs) == 1:
            return subs[0]
        return tuple(jnp.concatenate(parts, axis=0) for parts in zip(*subs))

    def sub_weights(r0, r_end, s1, s2):
        keys = r0 + CHUNK
        qq = r0 + lax.broadcasted_iota(jnp.int32, (CHUNK, CHUNK), 0)
        kk = lax.broadcasted_iota(jnp.int32, (CHUNK, CHUNK), 1)

        def masked_probs(s):
            parts = []
            for c0 in range(0, keys, CHUNK):
                piece = s[:, c0:c0 + CHUNK]
                cond = None
                if c0 == 0:
                    cond = kk >= PAD
                    if r0 == 0:
                        cond = cond | (kk == qq)
                if c0 + CHUNK > r0:
                    causal = (kk + c0) <= qq
                    cond = causal if cond is None else (cond & causal)
                if cond is not None:
                    piece = jnp.where(cond, piece, NEG)
                parts.append(piece)
            s = parts[0] if len(parts) == 1 else jnp.concatenate(parts, axis=1)
            p = jnp.exp2(s - jnp.max(s, axis=-1, keepdims=True))
            return p, jnp.sum(p, axis=-1, keepdims=True)

        p1, l1 = masked_probs(s1)
        p2, l2 = masked_probs(s2)
        a = (p1 - p2 * (lam * l1 * (1.0 / l2))).astype(BF16)
        if r_end > keys:
            a = jnp.concatenate([a, jnp.zeros((CHUNK, r_end - keys), BF16)], axis=1)
        return a, 1.0 / l1

    def values(r0, r1, a, inv_l1):
        o = jnp.dot(a, v_ref[0:r1, :], preferred_element_type=F32) * inv_l1
        y = o * lax.rsqrt(jnp.mean(o * o, axis=-1, keepdims=True) + EPS) * g
        o_ref[r0:r1, :] = (y * (1.0 - lam_init)).astype(BF16)

    nq = len(q_blocks)
    s_live, a_live = {}, {}
    for i in range(nq + 2):
        if i < nq:
            s_live[i] = scores(*q_blocks[i])
        if i >= 2:
            values(*q_blocks[i - 2], *a_live.pop(i - 2))
        if 1 <= i <= nq:
            a_live[i - 1] = weights(*q_blocks[i - 1], s_live.pop(i - 1))


def _attention(qk, v, lamp, subln_g, layer, B, L):
    T = B * L
    lam_init = 0.8 - 0.6 * math.exp(-0.3 * layer)
    q_blocks = ((0, CHUNK),) + tuple((r, r + TQ) for r in range(CHUNK, L, TQ))
    kh = A_QK // LANES
    return pl.pallas_call(
        functools.partial(_attn_kernel, lam_init, q_blocks),
        out_shape=jax.ShapeDtypeStruct((T, A_VW), BF16),
        grid=(B, A_HEADS),
        in_specs=[
            pl.BlockSpec((L, LANES), lambda b, h: (b, h)),
            pl.BlockSpec((L, LANES), lambda b, h: (b, kh + h)),
            pl.BlockSpec((L, A_V_DIM), lambda b, h: (b, h)),
            pl.BlockSpec((4, A_HEAD_DIM), lambda b, h: (0, 0)),
            pl.BlockSpec((1, A_V_DIM), lambda b, h: (0, 0)),
        ],
        out_specs=pl.BlockSpec((L, A_V_DIM), lambda b, h: (b, h)),
        compiler_params=_params("arbitrary", "arbitrary"),
        name="diff_attention",
    )(qk, qk, v, lamp, subln_g.reshape(1, A_V_DIM))


def _retention_kernel(nc, q_ref, k_ref, v_ref, rg_ref, gn_ref, decay_ref, tail_ref, qdec_ref,
                      cdec_ref, o_ref):
    nt = (((1,), (1,)), ((), ()))
    tn = (((0,), (0,)), ((), ()))
    states = [None] * RET_HEADS_PER_STEP
    for c in range(nc):
        rows = slice(c * CHUNK, (c + 1) * CHUNK)
        for j in range(RET_HEADS_PER_STEP):
            qk_cols = slice(j * R_QK_DIM, (j + 1) * R_QK_DIM)
            v_cols = slice(j * R_V_DIM, (j + 1) * R_V_DIM)
            qc = q_ref[rows, qk_cols]
            kc = k_ref[rows, qk_cols]
            vc = v_ref[rows, v_cols]
            s = lax.dot_general(qc.astype(BF16), kc.astype(BF16), nt, preferred_element_type=F32) * decay_ref[j]
            o = jnp.dot(s.astype(BF16), vc, preferred_element_type=F32)
            if states[j] is not None:
                o = o + jnp.dot((qc * qdec_ref[j]).astype(BF16), states[j].astype(BF16),
                                preferred_element_type=F32)
            if c + 1 < nc:
                kv = lax.dot_general((kc * tail_ref[j]).astype(BF16), vc, tn, preferred_element_type=F32)
                states[j] = kv if states[j] is None else cdec_ref[j] * states[j] + kv
            mu = jnp.mean(o, axis=-1, keepdims=True)
            d = o - mu
            var = jnp.mean(d * d, axis=-1, keepdims=True)
            y = d * lax.rsqrt(var + EPS) * gn_ref[:, v_cols]
            rg = rg_ref[rows, v_cols]
            o_ref[rows, v_cols] = (y * (rg * _sigmoid(rg))).astype(BF16)


def _retention(rqk, rv, gates, gn_g, tabs, B, L):
    T = B * L
    decay, tail, qdec, cdec = tabs
    hp = RET_HEADS_PER_STEP
    kh = R_QK // (hp * R_QK_DIM)
    return pl.pallas_call(
        functools.partial(_retention_kernel, L // CHUNK),
        out_shape=jax.ShapeDtypeStruct((T, R_VW), BF16),
        grid=(B, R_HEADS // hp),
        in_specs=[
            pl.BlockSpec((L, hp * R_QK_DIM), lambda b, h: (b, h)),
            pl.BlockSpec((L, hp * R_QK_DIM), lambda b, h: (b, kh + h)),
            pl.BlockSpec((L, hp * R_V_DIM), lambda b, h: (b, h)),
            pl.BlockSpec((L, hp * R_V_DIM), lambda b, h: (b, h)),
            pl.BlockSpec((1, hp * R_V_DIM), lambda b, h: (0, h)),
            pl.BlockSpec((hp, CHUNK, CHUNK), lambda b, h: (h, 0, 0)),
            pl.BlockSpec((hp, CHUNK, R_QK_DIM), lambda b, h: (h, 0, 0)),
            pl.BlockSpec((hp, CHUNK, R_QK_DIM), lambda b, h: (h, 0, 0)),
            pl.BlockSpec((hp, 1, R_V_DIM), lambda b, h: (h, 0, 0)),
        ],
        out_specs=pl.BlockSpec((L, hp * R_V_DIM), lambda b, h: (b, h)),
        compiler_params=_params("arbitrary", "arbitrary"),
        name="retention",
    )(rqk, rqk, rv, gates, gn_g.reshape(1, R_VW), decay, tail, qdec, cdec)


def _merge_kernel(n_chunks, oa_ref, ob_ref, wa_ref, wb_ref, ga_ref, gb_ref, o_ref, was_ref, wbs_ref):
    s = pl.program_id(0)
    ck = wa_ref.shape[1]

    @pl.when(s < n_chunks)
    def _():
        col = pl.multiple_of(s * ck, ck)
        was_ref[:, pl.ds(col, ck)] = wa_ref[...].astype(BF16)

    @pl.when(jnp.logical_and(s >= n_chunks, s < 2 * n_chunks))
    def _():
        col = pl.multiple_of((s - n_chunks) * ck, ck)
        wbs_ref[:, pl.ds(col, ck)] = wb_ref[...].astype(BF16)

    @pl.when(s >= 2 * n_chunks)
    def _():
        def matmul(i):
            x_ref, ws_ref = ((oa_ref, was_ref), (ob_ref, wbs_ref))[i]
            return jnp.dot(x_ref[...], ws_ref[...], preferred_element_type=F32)

        gated = []

        def gate(i, branch):
            gated.append(_sigmoid((ga_ref, gb_ref)[i][...]) * branch)

        _software_pipeline(2, matmul, gate)
        o_ref[...] = (gated[0] + gated[1]).astype(BF16)


def _merge(oa, ob, w_a, w_b, gates, layer):
    T = oa.shape[0]
    tm, ck = TM_RESIDENT, STAGE_COLS
    nck = D_MODEL // ck
    row = lambda s: (jnp.maximum(s - 2 * nck, 0), 0)
    ga0 = (C_GA - C_RG) // D_MODEL
    gb0 = (C_GB - C_RG) // D_MODEL
    return pl.pallas_call(
        functools.partial(_merge_kernel, nck),
        out_shape=jax.ShapeDtypeStruct((T, D_MODEL), BF16),
        grid=(2 * nck + T // tm,),
        in_specs=[
            pl.BlockSpec((tm, A_VW), row),
            pl.BlockSpec((tm, R_VW), row),
            pl.BlockSpec((None, A_VW, ck), lambda s: (layer, 0, jnp.minimum(s, nck - 1))),
            pl.BlockSpec((None, R_VW, ck), lambda s: (layer, 0, jnp.clip(s - nck, 0, nck - 1))),
            pl.BlockSpec((tm, D_MODEL), lambda s: (jnp.maximum(s - 2 * nck, 0), ga0)),
            pl.BlockSpec((tm, D_MODEL), lambda s: (jnp.maximum(s - 2 * nck, 0), gb0)),
        ],
        out_specs=pl.BlockSpec((tm, D_MODEL), row),
        scratch_shapes=[pltpu.VMEM((A_VW, D_MODEL), BF16), pltpu.VMEM((R_VW, D_MODEL), BF16)],
        compiler_params=_params("arbitrary"),
        name="merge",
    )(oa, ob, w_a, w_b, gates, gates)


def _proj_ln_kernel(n_chunks, row_split, a_ref, w_ref, h_ref, g_ref, b_ref, of_ref, *rest):
    ob_ref, ws_ref = rest if len(rest) == 2 else (None, rest[0])
    s = pl.program_id(0)
    ck = w_ref.shape[1]

    @pl.when(s < n_chunks)
    def _():
        col = pl.multiple_of(s * ck, ck)
        ws_ref[:, pl.ds(col, ck)] = w_ref[...].astype(BF16)

    @pl.when(s >= n_chunks)
    def _():
        rb = a_ref.shape[0] // row_split

        def matmul(r):
            return jnp.dot(a_ref[r * rb:(r + 1) * rb, :], ws_ref[...], preferred_element_type=F32)

        def residual_ln(r, acc):
            rows = slice(r * rb, (r + 1) * rb)
            y = _layer_norm_rows(ALPHA * h_ref[rows, :] + acc, g_ref[...], b_ref[...])
            of_ref[rows, :] = y
            if ob_ref is not None:
                ob_ref[rows, :] = y.astype(BF16)

        _software_pipeline(row_split, matmul, residual_ln)


def _proj_ln(a, kdim, w, layer, hf, g, b, tm, ck, name):
    T = hf.shape[0]
    nck = D_MODEL // ck
    row = lambda s: (jnp.maximum(s - nck, 0), 0)
    return pl.pallas_call(
        functools.partial(_proj_ln_kernel, nck, tm // MIN_MATMUL_ROWS),
        out_shape=(jax.ShapeDtypeStruct((T, D_MODEL), F32), jax.ShapeDtypeStruct((T, D_MODEL), BF16)),
        grid=(nck + T // tm,),
        in_specs=[
            pl.BlockSpec((tm, kdim), row),
            pl.BlockSpec((None, kdim, ck), lambda s: (layer, 0, jnp.minimum(s, nck - 1))),
            pl.BlockSpec((tm, D_MODEL), row),
            pl.BlockSpec((None, 1, D_MODEL), lambda s: (layer, 0, 0)),
            pl.BlockSpec((None, 1, D_MODEL), lambda s: (layer, 0, 0)),
        ],
        out_specs=(pl.BlockSpec((tm, D_MODEL), row), pl.BlockSpec((tm, D_MODEL), row)),
        scratch_shapes=[pltpu.VMEM((kdim, D_MODEL), BF16)],
        compiler_params=_params("arbitrary"),
        name=name,
    )(a, w, hf, g.reshape(DEPTH, 1, D_MODEL), b.reshape(DEPTH, 1, D_MODEL))


def _proj_ln_final(a, kdim, w, layer, hf, g, b, tm, ck, nb, name):
    T = hf.shape[0]
    B = T // (nb * CHUNK)
    seq = (nb - 1) * CHUNK
    nck = D_MODEL // ck
    per_seq = seq // tm

    def rows(cols):
        def index(s):
            t = jnp.maximum(s - nck, 0)
            return ((t // per_seq) * nb + 1 + (t % per_seq) * (tm // CHUNK)) * CHUNK, 0
        return pl.BlockSpec((pl.Element(tm), pl.Element(cols)), index)

    return pl.pallas_call(
        functools.partial(_proj_ln_kernel, nck, 1),
        out_shape=jax.ShapeDtypeStruct((B * seq, D_MODEL), F32),
        grid=(nck + B * per_seq,),
        in_specs=[
            rows(kdim),
            pl.BlockSpec((None, kdim, ck), lambda s: (layer, 0, jnp.minimum(s, nck - 1))),
            rows(D_MODEL),
            pl.BlockSpec((None, 1, D_MODEL), lambda s: (layer, 0, 0)),
            pl.BlockSpec((None, 1, D_MODEL), lambda s: (layer, 0, 0)),
        ],
        out_specs=pl.BlockSpec((tm, D_MODEL), lambda s: (jnp.maximum(s - nck, 0), 0)),
        scratch_shapes=[pltpu.VMEM((kdim, D_MODEL), BF16)],
        compiler_params=_params("arbitrary"),
        name=name,
    )(a, w, hf, g.reshape(DEPTH, 1, D_MODEL), b.reshape(DEPTH, 1, D_MODEL))


def _ffn_up_kernel(rb, overlap, h_ref, wg_ref, wv_ref, cwg_ref, cwv_ref, cbg_ref, cbv_ref, o_ref,
                   wgs_ref, wvs_ref):
    tn = o_ref.shape[1]
    last_col_tile = pl.program_id(0) == pl.num_programs(0) - 1

    def placed(w):
        moved = jnp.concatenate([w[:, overlap:], jnp.zeros((w.shape[0], overlap), w.dtype)], axis=1)
        return jnp.where(last_col_tile, moved, w)

    @pl.when(pl.program_id(1) == 0)
    def _():
        wgs_ref[...] = placed(wg_ref[...].astype(BF16))
        wvs_ref[...] = placed(wv_ref[...].astype(BF16))

    row8 = lax.broadcasted_iota(jnp.int32, (HALO, tn), 0)
    halves = ((wgs_ref, cwg_ref, cbg_ref), (wvs_ref, cwv_ref, cbv_ref))
    prevs = [jnp.zeros((HALO, tn), F32)] * 2
    gates = {}

    n_rows = h_ref.shape[0]
    stages = []
    for r0 in range(0, n_rows, rb):
        stages.append((r0, r0 + rb, 0))
        if r0 + rb < n_rows:
            stages.append((r0, r0 + rb, 1))
        else:
            stages += [(r0, r0 + rb // 2, 1), (r0 + rb // 2, r0 + rb, 1)]

    def matmul(i):
        r0, r1, half = stages[i]
        u = jnp.dot(h_ref[r0:r1, :], halves[half][0][...], preferred_element_type=F32)
        if r0 == 0:
            u = jnp.concatenate([jnp.zeros((PAD, tn), F32), u[PAD:]], axis=0)
        return u

    def shifted(u, prev, k):
        rolled = pltpu.roll(u, k, 0)
        head = rolled[:HALO]
        for t in range(k):
            head = jnp.where(row8 == t, prev[HALO - k + t:HALO - k + t + 1], head)
        return jnp.concatenate([head, rolled[HALO:]], axis=0)

    def conv_geglu(i, u):
        r0, r1, half = stages[i]
        _, cw_ref, cb_ref = halves[half]
        prev = prevs[half]
        prevs[half] = u[r1 - r0 - HALO:r1 - r0, :]
        scale = 1.0 if half == 0 else 0.5
        cw = placed(cw_ref[...]) * scale
        c = (shifted(u, prev, 2) * cw[0:1] + shifted(u, prev, 1) * cw[1:2] + u * cw[2:3]
             + placed(cb_ref[...]) * scale)
        if half == 0:
            gates[r0] = c
        else:
            g0 = (r0 // rb) * rb
            gate = gates[g0][r0 - g0:r1 - g0, :]
            o_ref[r0:r1, :] = (gate * (1.0 + lax.erf(gate * (2.0 ** -0.5))) * c).astype(BF16)

    _software_pipeline(len(stages), matmul, conv_geglu)


def _ffn_up(hb, w_up, conv_w, conv_b, layer, L):
    T = hb.shape[0]
    tm, tn = L, TN_FF
    assert PAD <= TM_BIG and L % TM_BIG == 0
    nt = pl.cdiv(D_FF, tn)
    overlap = nt * tn - D_FF
    lane_tile = lambda n: jnp.minimum(n * (tn // LANES), (D_FF - tn) // LANES)

    def cols(rows, half):
        return pl.BlockSpec((None, pl.Element(rows), pl.Element(tn)),
                            lambda n, m: (layer, 0, (lane_tile(n) + half * (D_FF // LANES)) * LANES))

    return pl.pallas_call(
        functools.partial(_ffn_up_kernel, TM_BIG, overlap),
        out_shape=jax.ShapeDtypeStruct((T, nt * tn), BF16),
        grid=(nt, T // tm),
        in_specs=[
            pl.BlockSpec((tm, D_MODEL), lambda n, m: (m, 0)),
            cols(D_MODEL, 0), cols(D_MODEL, 1),
            cols(CONV_W, 0), cols(CONV_W, 1),
            cols(1, 0), cols(1, 1),
        ],
        out_specs=pl.BlockSpec((tm, tn), lambda n, m: (m, n)),
        scratch_shapes=[pltpu.VMEM((D_MODEL, tn), BF16), pltpu.VMEM((D_MODEL, tn), BF16)],
        compiler_params=_params("arbitrary", "arbitrary"),
        name="ffn_up_conv_geglu",
    )(hb, w_up, w_up, conv_w, conv_w, conv_b, conv_b)


def _const(a):
    return jnp.asarray(np.ascontiguousarray(a, dtype=np.float32))


def _rotary_tables(L):
    pos = (np.arange(L) - PAD).astype(np.float64)
    a_freq = 1.0 / (ROPE_THETA ** (np.arange(0, A_HEAD_DIM, 2, dtype=np.float64) / A_HEAD_DIM))
    ang = pos[:, None] * a_freq[None, :]
    ang = np.concatenate([ang] * (LANES // (A_HEAD_DIM // 2)), axis=-1)
    lane = np.arange(LANES)
    sign_a = np.where((lane & (A_HEAD_DIM // 2)) == 0, -1.0, 1.0)
    cos_a, sin_a = np.cos(ang), np.sin(ang) * sign_a
    scale = (A_HEAD_DIM ** -0.5) * math.log2(math.e)
    cos_a = np.stack([cos_a * scale, cos_a])
    sin_a = np.stack([sin_a * scale, sin_a])

    r_freq = 1.0 / (ROPE_THETA ** np.linspace(0.0, 1.0, R_QK_DIM // 2, dtype=np.float64))
    ang = pos[:, None] * r_freq[None, :]
    ang = np.concatenate([ang, ang], axis=-1)
    sign_r = np.where(lane < R_QK_DIM // 2, -1.0, 1.0)
    cos_r = np.broadcast_to(np.cos(ang), (2, L, LANES))
    sin_r = np.broadcast_to(np.sin(ang) * sign_r, (2, L, LANES))
    valid = (np.arange(L) >= PAD).astype(np.float64)
    post_k = np.broadcast_to(((R_QK_DIM ** -0.5) * valid)[:, None], (L, LANES))
    post_r = np.stack([np.ones((L, LANES)), post_k])
    return (_const(cos_a), _const(sin_a)), (_const(cos_r), _const(sin_r), _const(post_r))


def _retention_tables():
    log_gamma = np.log(1.0 - 2.0 ** (-5.0 - np.arange(R_HEADS, dtype=np.float64)))
    p = np.arange(CHUNK, dtype=np.float64)
    diff = p[:, None] - p[None, :]
    lg = log_gamma[:, None, None]
    decay = np.where(diff >= 0, np.exp(lg * np.maximum(diff, 0.0)), 0.0)
    tail = np.exp(log_gamma[:, None] * (CHUNK - 1 - p))
    qdec = np.exp(log_gamma[:, None] * (p + 1.0))
    tail = np.broadcast_to(tail[:, :, None], (R_HEADS, CHUNK, R_QK_DIM))
    qdec = np.broadcast_to(qdec[:, :, None], (R_HEADS, CHUNK, R_QK_DIM))
    cdec = np.broadcast_to(np.exp(log_gamma * CHUNK)[:, None, None], (R_HEADS, 1, R_V_DIM))
    return _const(decay), _const(tail), _const(qdec), _const(cdec)


def kernel(x, meta_tokens, ln_emb_g, ln_emb_b, w_in, lam_q1, lam_k1, lam_q2, lam_k2, subln_g, ret_gn_g,
           w_branch_a, w_branch_b, w_out, ln1_g, ln1_b, w_up, conv_w, conv_b, w_down, ln2_g, ln2_b):
    B, seq, _ = x.shape
    assert seq % CHUNK == 0 and x.shape[2] == D_MODEL and meta_tokens.shape == (N_META, D_MODEL)
    assert w_in.shape == (DEPTH, D_MODEL, W_IN_COLS) and w_up.shape == (DEPTH, D_MODEL, 2 * D_FF)
    nb = 1 + seq // CHUNK
    L = nb * CHUNK
    rot_a, rot_r = _rotary_tables(L)
    ret_tabs = _retention_tables()

    conv_b3 = conv_b.reshape(DEPTH, 1, 2 * D_FF)

    hf, hb = _embed_ln(x, meta_tokens, ln_emb_g, ln_emb_b, nb)
    for l in range(DEPTH):
        qk = _proj(hb, w_in, l, C_AQ, 2 * A_QK, "rot_a", BF16, rot_a, lambda n: n // (A_QK // TN_PROJ))
        av = _proj(hb, w_in, l, C_AV, A_VW, "plain", BF16)
        rqk = _proj(hb, w_in, l, C_RQ, 2 * R_QK, "rot_r", F32, rot_r, lambda n: n // (R_QK // TN_PROJ))
        rv = _proj(hb, w_in, l, C_RV, R_VW, "plain", BF16)
        gates = _proj(hb, w_in, l, C_RG, R_VW + 2 * D_MODEL, "plain", F32)

        lamp = jnp.stack([lam_q1[l], lam_k1[l], lam_q2[l], lam_k2[l]])
        oa = _attention(qk, av, lamp, subln_g[l], l, B, L)
        ob = _retention(rqk, rv, gates, ret_gn_g[l], ret_tabs, B, L)
        merged = _merge(oa, ob, w_branch_a, w_branch_b, gates, l)
        hf, hb = _proj_ln(merged, D_MODEL, w_out, l, hf, ln1_g, ln1_b, TM_SMALL, STAGE_COLS, "out_proj_ln")

        gact = _ffn_up(hb, w_up, conv_w, conv_b3, l, L)
        if l + 1 < DEPTH:
            hf, hb = _proj_ln(gact, D_FF, w_down, l, hf, ln2_g, ln2_b, TM_RESIDENT, STAGE_COLS_DOWN, "ffn_down_ln")
        else:
            out = _proj_ln_final(gact, D_FF, w_down, l, hf, ln2_g, ln2_b, TM_FINAL, STAGE_COLS_DOWN, nb,
                                 "ffn_down_ln_out")
    return out.reshape(B, seq, D_MODEL)
```

```python
import functools
import math

import jax
import jax.numpy as jnp
import numpy as np
from jax import lax
from jax.experimental import pallas as pl
from jax.experimental.pallas import tpu as pltpu

F32 = jnp.float32
BF16 = jnp.bfloat16

D_MODEL = 2048
DEPTH = 2
N_META = 16
CHUNK = 128
PAD = CHUNK - N_META
A_HEAD_DIM = 64
A_V_DIM = 2 * A_HEAD_DIM
A_HEADS = D_MODEL // A_V_DIM
A_QK = A_HEADS * 2 * A_HEAD_DIM
A_VW = A_HEADS * A_V_DIM
R_QK_DIM = 128
R_V_DIM = 2 * R_QK_DIM
R_HEADS = D_MODEL // R_V_DIM
R_QK = R_HEADS * R_QK_DIM
R_VW = R_HEADS * R_V_DIM
D_FF = 128 * ((8 * D_MODEL // 3 + 127) // 128)
CONV_W = 3
ROPE_THETA = 10000.0
ALPHA = (2 * DEPTH) ** 0.25
EPS = 1e-5
NEG = -1e30

LANES = 128
VMEM_LIMIT = 56 * 1024 * 1024

C_AQ = 0
C_AK = C_AQ + A_QK
C_AV = C_AK + A_QK
C_RQ = C_AV + A_VW
C_RK = C_RQ + R_QK
C_RV = C_RK + R_QK
C_RG = C_RV + R_VW
C_GA = C_RG + R_VW
C_GB = C_GA + D_MODEL
W_IN_COLS = C_GB + D_MODEL

TM_BIG = 1088
TM_SMALL = 544
TM_RESIDENT = 272
TM_FINAL = 2 * CHUNK
TN_PROJ = 1024
TN_FF = 512
STAGE_COLS = 512
STAGE_COLS_DOWN = 512
TQ = 256
RET_HEADS_PER_STEP = 2
ROW_SPLIT = 4
MIN_MATMUL_ROWS = 272
HALO = 8


def _params(*sem):
    return pltpu.CompilerParams(dimension_semantics=sem, vmem_limit_bytes=VMEM_LIMIT)


def _layer_norm_rows(z, g, b):
    mu = jnp.mean(z, axis=-1, keepdims=True)
    d = z - mu
    var = jnp.mean(d * d, axis=-1, keepdims=True)
    return d * lax.rsqrt(var + EPS) * g + b


def _sigmoid(x):
    return 1.0 / (1.0 + jnp.exp(-x))


def _software_pipeline(n, produce, consume):
    nxt = produce(0)
    for i in range(n):
        cur = nxt
        if i + 1 < n:
            nxt = produce(i + 1)
        consume(i, cur)


def _embed_ln_kernel(x_ref, meta_ref, g_ref, b_ref, hf_ref, hb_ref):
    g = g_ref[...]
    b = b_ref[...]
    tm = hf_ref.shape[0]

    def emit(rows):
        y = _layer_norm_rows(rows, g, b)
        hf_ref[...] = y
        hb_ref[...] = y.astype(BF16)

    @pl.when(pl.program_id(1) == 0)
    def _():
        emit(jnp.concatenate([jnp.zeros((PAD, D_MODEL), F32), meta_ref[...], x_ref[0:tm - CHUNK, :]], axis=0))

    @pl.when(pl.program_id(1) > 0)
    def _():
        emit(x_ref[...])


def _embed_ln(x, meta, g, b, nb):
    B, seq, _ = x.shape
    L = nb * CHUNK
    tm = TM_BIG
    per_seq = L // tm
    step = 32
    assert tm % step == 0 and CHUNK % step == 0 and L % tm == 0

    def token_rows(bi, i):
        return (bi * (seq // step) + jnp.maximum(i * (tm // step) - CHUNK // step, 0)) * step, 0

    return pl.pallas_call(
        _embed_ln_kernel,
        out_shape=(jax.ShapeDtypeStruct((B * L, D_MODEL), F32), jax.ShapeDtypeStruct((B * L, D_MODEL), BF16)),
        grid=(B, per_seq),
        in_specs=[
            pl.BlockSpec((pl.Element(tm), pl.Element(D_MODEL)), token_rows),
            pl.BlockSpec((N_META, D_MODEL), lambda bi, i: (0, 0)),
            pl.BlockSpec((1, D_MODEL), lambda bi, i: (0, 0)),
            pl.BlockSpec((1, D_MODEL), lambda bi, i: (0, 0)),
        ],
        out_specs=(
            pl.BlockSpec((tm, D_MODEL), lambda bi, i: (bi * per_seq + i, 0)),
            pl.BlockSpec((tm, D_MODEL), lambda bi, i: (bi * per_seq + i, 0)),
        ),
        compiler_params=_params("arbitrary", "arbitrary"),
        name="embed_ln",
    )(x.reshape(B * seq, D_MODEL), meta, g.reshape(1, D_MODEL), b.reshape(1, D_MODEL))


def _proj_kernel(kind, h_ref, w_ref, *rest):
    if kind == "plain":
        o_ref, wb_ref = rest
    elif kind == "rot_a":
        cos_ref, sin_ref, o_ref, wb_ref = rest
    else:
        cos_ref, sin_ref, post_ref, o_ref, wb_ref = rest

    @pl.when(pl.program_id(1) == 0)
    def _():
        wb_ref[...] = w_ref[...].astype(BF16)

    tm, tn = o_ref.shape
    rb = tm // ROW_SPLIT
    if kind == "rot_a":
        lane = lax.broadcasted_iota(jnp.int32, (rb, LANES), 1)
        low_half = (lane & (A_HEAD_DIM // 2)) == 0

    def matmul(r):
        return jnp.dot(h_ref[r * rb:(r + 1) * rb, :], wb_ref[...], preferred_element_type=F32)

    def epilogue(r, acc):
        rows = slice(r * rb, (r + 1) * rb)
        if kind == "plain":
            o_ref[rows, :] = acc.astype(o_ref.dtype)
            return
        cos = cos_ref[rows, :]
        sin = sin_ref[rows, :]
        for j in range(tn // LANES):
            t = acc[:, j * LANES:(j + 1) * LANES]
            if kind == "rot_a":
                rot = jnp.where(low_half, pltpu.roll(t, LANES - A_HEAD_DIM // 2, 1),
                                pltpu.roll(t, A_HEAD_DIM // 2, 1))
                y = t * cos + rot * sin
            else:
                y = (t * cos + pltpu.roll(t, R_QK_DIM // 2, 1) * sin) * post_ref[rows, :]
            o_ref[rows, j * LANES:(j + 1) * LANES] = y.astype(o_ref.dtype)

    _software_pipeline(ROW_SPLIT, matmul, epilogue)


def _proj(hb, w_in, layer, col0, ncols, kind, out_dtype, tables=(), variant=None):
    T = hb.shape[0]
    tm, tn = TM_BIG, TN_PROJ
    nt, mt = ncols // tn, T // tm
    tile0 = col0 // tn
    in_specs = [
        pl.BlockSpec((tm, D_MODEL), lambda n, m: (m, 0)),
        pl.BlockSpec((None, D_MODEL, tn), lambda n, m: (layer, 0, tile0 + n)),
    ]
    for _ in tables:
        in_specs.append(pl.BlockSpec((None, tm, LANES), lambda n, m: (variant(n), m % 2, 0)))
    return pl.pallas_call(
        functools.partial(_proj_kernel, kind),
        out_shape=jax.ShapeDtypeStruct((T, ncols), out_dtype),
        grid=(nt, mt),
        in_specs=in_specs,
        out_specs=pl.BlockSpec((tm, tn), lambda n, m: (m, n)),
        scratch_shapes=[pltpu.VMEM((D_MODEL, tn), BF16)],
        compiler_params=_params("arbitrary", "arbitrary"),
        name="proj_" + kind,
    )(hb, w_in, *tables)


def _attn_kernel(lam_init, q_blocks, q_ref, k_ref, v_ref, lamp_ref, g_ref, o_ref):
    lp = lamp_ref[...]
    lam = (jnp.exp(jnp.sum(lp[0:1] * lp[1:2], axis=-1, keepdims=True))
           - jnp.exp(jnp.sum(lp[2:3] * lp[3:4], axis=-1, keepdims=True)) + lam_init)
    lane = lax.broadcasted_iota(jnp.int32, (1, LANES), 1)
    map1 = jnp.where(lane < A_HEAD_DIM, 1.0, 0.0).astype(BF16)
    map2 = jnp.where(lane >= A_HEAD_DIM, 1.0, 0.0).astype(BF16)
    g = g_ref[...]
    nt = (((1,), (1,)), ((), ()))

    def scores(r0, r1):
        q = q_ref[r0:r1, :]
        return lax.dot_general(jnp.concatenate([q * map1, q * map2], axis=0), k_ref[0:r1, :], nt,
                               preferred_element_type=F32)

    def weights(r0, r1, s12):
        tq = r1 - r0
        subs = [sub_weights(r0 + t, r1, s12[t:t + CHUNK], s12[tq + t:tq + t + CHUNK])
                for t in range(0, tq, CHUNK)]
        if len(subs) == 1:
            return subs[0]
        return tuple(jnp.concatenate(parts, axis=0) for parts in zip(*subs))

    def sub_weights(r0, r_end, s1, s2):
        keys = r0 + CHUNK
        qq = r0 + lax.broadcasted_iota(jnp.int32, (CHUNK, CHUNK), 0)
        kk = lax.broadcasted_iota(jnp.int32, (CHUNK, CHUNK), 1)

        def masked_probs(s):
            parts = []
            for c0 in range(0, keys, CHUNK):
                piece = s[:, c0:c0 + CHUNK]
                cond = None
                if c0 == 0:
                    cond = kk >= PAD
                    if r0 == 0:
                        cond = cond | (kk == qq)
                if c0 + CHUNK > r0:
                    causal = (kk + c0) <= qq
                    cond = causal if cond is None else (cond & causal)
                if cond is not None:
                    piece = jnp.where(cond, piece, NEG)
                parts.append(piece)
            s = parts[0] if len(parts) == 1 else jnp.concatenate(parts, axis=1)
            p = jnp.exp2(s - jnp.max(s, axis=-1, keepdims=True))
            return p, jnp.sum(p, axis=-1, keepdims=True)

        p1, l1 = masked_probs(s1)
        p2, l2 = masked_probs(s2)
        a = (p1 - p2 * (lam * l1 * (1.0 / l2))).astype(BF16)
        if r_end > keys:
            a = jnp.concatenate([a, jnp.zeros((CHUNK, r_end - keys), BF16)], axis=1)
        return a, 1.0 / l1

    def values(r0, r1, a, inv_l1):
        o = jnp.dot(a, v_ref[0:r1, :], preferred_element_type=F32) * inv_l1
        y = o * lax.rsqrt(jnp.mean(o * o, axis=-1, keepdims=True) + EPS) * g
        o_ref[r0:r1, :] = (y * (1.0 - lam_init)).astype(BF16)

    nq = len(q_blocks)
    s_live, a_live = {}, {}
    for i in range(nq + 2):
        if i < nq:
            s_live[i] = scores(*q_blocks[i])
        if i >= 2:
            values(*q_blocks[i - 2], *a_live.pop(i - 2))
        if 1 <= i <= nq:
            a_live[i - 1] = weights(*q_blocks[i - 1], s_live.pop(i - 1))


def _attention(qk, v, lamp, subln_g, layer, B, L):
    T = B * L
    lam_init = 0.8 - 0.6 * math.exp(-0.3 * layer)
    q_blocks = ((0, CHUNK),) + tuple((r, r + TQ) for r in range(CHUNK, L, TQ))
    kh = A_QK // LANES
    return pl.pallas_call(
        functools.partial(_attn_kernel, lam_init, q_blocks),
        out_shape=jax.ShapeDtypeStruct((T, A_VW), BF16),
        grid=(B, A_HEADS),
        in_specs=[
            pl.BlockSpec((L, LANES), lambda b, h: (b, h)),
            pl.BlockSpec((L, LANES), lambda b, h: (b, kh + h)),
            pl.BlockSpec((L, A_V_DIM), lambda b, h: (b, h)),
            pl.BlockSpec((4, A_HEAD_DIM), lambda b, h: (0, 0)),
            pl.BlockSpec((1, A_V_DIM), lambda b, h: (0, 0)),
        ],
        out_specs=pl.BlockSpec((L, A_V_DIM), lambda b, h: (b, h)),
        compiler_params=_params("arbitrary", "arbitrary"),
        name="diff_attention",
    )(qk, qk, v, lamp, subln_g.reshape(1, A_V_DIM))


def _retention_kernel(nc, q_ref, k_ref, v_ref, rg_ref, gn_ref, decay_ref, tail_ref, qdec_ref,
                      cdec_ref, o_ref):
    nt = (((1,), (1,)), ((), ()))
    tn = (((0,), (0,)), ((), ()))
    states = [None] * RET_HEADS_PER_STEP
    for c in range(nc):
        rows = slice(c * CHUNK, (c + 1) * CHUNK)
        for j in range(RET_HEADS_PER_STEP):
            qk_cols = slice(j * R_QK_DIM, (j + 1) * R_QK_DIM)
            v_cols = slice(j * R_V_DIM, (j + 1) * R_V_DIM)
            qc = q_ref[rows, qk_cols]
            kc = k_ref[rows, qk_cols]
            vc = v_ref[rows, v_cols]
            s = lax.dot_general(qc.astype(BF16), kc.astype(BF16), nt, preferred_element_type=F32) * decay_ref[j]
            o = jnp.dot(s.astype(BF16), vc, preferred_element_type=F32)
            if states[j] is not None:
                o = o + jnp.dot((qc * qdec_ref[j]).astype(BF16), states[j].astype(BF16),
                                preferred_element_type=F32)
            if c + 1 < nc:
                kv = lax.dot_general((kc * tail_ref[j]).astype(BF16), vc, tn, preferred_element_type=F32)
                states[j] = kv if states[j] is None else cdec_ref[j] * states[j] + kv
            mu = jnp.mean(o, axis=-1, keepdims=True)
            d = o - mu
            var = jnp.mean(d * d, axis=-1, keepdims=True)
            y = d * lax.rsqrt(var + EPS) * gn_ref[:, v_cols]
            rg = rg_ref[rows, v_cols]
            o_ref[rows, v_cols] = (y * (rg * _sigmoid(rg))).astype(BF16)


def _retention(rqk, rv, gates, gn_g, tabs, B, L):
    T = B * L
    decay, tail, qdec, cdec = tabs
    hp = RET_HEADS_PER_STEP
    kh = R_QK // (hp * R_QK_DIM)
    return pl.pallas_call(
        functools.partial(_retention_kernel, L // CHUNK),
        out_shape=jax.ShapeDtypeStruct((T, R_VW), BF16),
        grid=(B, R_HEADS // hp),
        in_specs=[
            pl.BlockSpec((L, hp * R_QK_DIM), lambda b, h: (b, h)),
            pl.BlockSpec((L, hp * R_QK_DIM), lambda b, h: (b, kh + h)),
            pl.BlockSpec((L, hp * R_V_DIM), lambda b, h: (b, h)),
            pl.BlockSpec((L, hp * R_V_DIM), lambda b, h: (b, h)),
            pl.BlockSpec((1, hp * R_V_DIM), lambda b, h: (0, h)),
            pl.BlockSpec((hp, CHUNK, CHUNK), lambda b, h: (h, 0, 0)),
            pl.BlockSpec((hp, CHUNK, R_QK_DIM), lambda b, h: (h, 0, 0)),
            pl.BlockSpec((hp, CHUNK, R_QK_DIM), lambda b, h: (h, 0, 0)),
            pl.BlockSpec((hp, 1, R_V_DIM), lambda b, h: (h, 0, 0)),
        ],
        out_specs=pl.BlockSpec((L, hp * R_V_DIM), lambda b, h: (b, h)),
        compiler_params=_params("arbitrary", "arbitrary"),
        name="retention",
    )(rqk, rqk, rv, gates, gn_g.reshape(1, R_VW), decay, tail, qdec, cdec)


def _merge_kernel(n_chunks, oa_ref, ob_ref, wa_ref, wb_ref, ga_ref, gb_ref, o_ref, was_ref, wbs_ref):
    s = pl.program_id(0)
    ck = wa_ref.shape[1]

    @pl.when(s < n_chunks)
    def _():
        col = pl.multiple_of(s * ck, ck)
        was_ref[:, pl.ds(col, ck)] = wa_ref[...].astype(BF16)

    @pl.when(jnp.logical_and(s >= n_chunks, s < 2 * n_chunks))
    def _():
        col = pl.multiple_of((s - n_chunks) * ck, ck)
        wbs_ref[:, pl.ds(col, ck)] = wb_ref[...].astype(BF16)

    @pl.when(s >= 2 * n_chunks)
    def _():
        def matmul(i):
            x_ref, ws_ref = ((oa_ref, was_ref), (ob_ref, wbs_ref))[i]
            return jnp.dot(x_ref[...], ws_ref[...], preferred_element_type=F32)

        gated = []

        def gate(i, branch):
            gated.append(_sigmoid((ga_ref, gb_ref)[i][...]) * branch)

        _software_pipeline(2, matmul, gate)
        o_ref[...] = (gated[0] + gated[1]).astype(BF16)


def _merge(oa, ob, w_a, w_b, gates, layer):
    T = oa.shape[0]
    tm, ck = TM_RESIDENT, STAGE_COLS
    nck = D_MODEL // ck
    row = lambda s: (jnp.maximum(s - 2 * nck, 0), 0)
    ga0 = (C_GA - C_RG) // D_MODEL
    gb0 = (C_GB - C_RG) // D_MODEL
    return pl.pallas_call(
        functools.partial(_merge_kernel, nck),
        out_shape=jax.ShapeDtypeStruct((T, D_MODEL), BF16),
        grid=(2 * nck + T // tm,),
        in_specs=[
            pl.BlockSpec((tm, A_VW), row),
            pl.BlockSpec((tm, R_VW), row),
            pl.BlockSpec((None, A_VW, ck), lambda s: (layer, 0, jnp.minimum(s, nck - 1))),
            pl.BlockSpec((None, R_VW, ck), lambda s: (layer, 0, jnp.clip(s - nck, 0, nck - 1))),
            pl.BlockSpec((tm, D_MODEL), lambda s: (jnp.maximum(s - 2 * nck, 0), ga0)),
            pl.BlockSpec((tm, D_MODEL), lambda s: (jnp.maximum(s - 2 * nck, 0), gb0)),
        ],
        out_specs=pl.BlockSpec((tm, D_MODEL), row),
        scratch_shapes=[pltpu.VMEM((A_VW, D_MODEL), BF16), pltpu.VMEM((R_VW, D_MODEL), BF16)],
        compiler_params=_params("arbitrary"),
        name="merge",
    )(oa, ob, w_a, w_b, gates, gates)


def _proj_ln_kernel(n_chunks, row_split, a_ref, w_ref, h_ref, g_ref, b_ref, of_ref, *rest):
    ob_ref, ws_ref = rest if len(rest) == 2 else (None, rest[0])
    s = pl.program_id(0)
    ck = w_ref.shape[1]

    @pl.when(s < n_chunks)
    def _():
        col = pl.multiple_of(s * ck, ck)
        ws_ref[:, pl.ds(col, ck)] = w_ref[...].astype(BF16)

    @pl.when(s >= n_chunks)
    def _():
        rb = a_ref.shape[0] // row_split

        def matmul(r):
            return jnp.dot(a_ref[r * rb:(r + 1) * rb, :], ws_ref[...], preferred_element_type=F32)

        def residual_ln(r, acc):
            rows = slice(r * rb, (r + 1) * rb)
            y = _layer_norm_rows(ALPHA * h_ref[rows, :] + acc, g_ref[...], b_ref[...])
            of_ref[rows, :] = y
            if ob_ref is not None:
                ob_ref[rows, :] = y.astype(BF16)

        _software_pipeline(row_split, matmul, residual_ln)


def _proj_ln(a, kdim, w, layer, hf, g, b, tm, ck, name):
    T = hf.shape[0]
    nck = D_MODEL // ck
    row = lambda s: (jnp.maximum(s - nck, 0), 0)
    return pl.pallas_call(
        functools.partial(_proj_ln_kernel, nck, tm // MIN_MATMUL_ROWS),
        out_shape=(jax.ShapeDtypeStruct((T, D_MODEL), F32), jax.ShapeDtypeStruct((T, D_MODEL), BF16)),
        grid=(nck + T // tm,),
        in_specs=[
            pl.BlockSpec((tm, kdim), row),
            pl.BlockSpec((None, kdim, ck), lambda s: (layer, 0, jnp.minimum(s, nck - 1)),
                         pipeline_mode=pl.Buffered(1)),
            pl.BlockSpec((tm, D_MODEL), row),
            pl.BlockSpec((None, 1, D_MODEL), lambda s: (layer, 0, 0)),
            pl.BlockSpec((None, 1, D_MODEL), lambda s: (layer, 0, 0)),
        ],
        out_specs=(pl.BlockSpec((tm, D_MODEL), row), pl.BlockSpec((tm, D_MODEL), row)),
        scratch_shapes=[pltpu.VMEM((kdim, D_MODEL), BF16)],
        compiler_params=_params("arbitrary"),
        name=name,
    )(a, w, hf, g.reshape(DEPTH, 1, D_MODEL), b.reshape(DEPTH, 1, D_MODEL))


def _proj_ln_final(a, kdim, w, layer, hf, g, b, tm, ck, nb, name):
    T = hf.shape[0]
    B = T // (nb * CHUNK)
    seq = (nb - 1) * CHUNK
    nck = D_MODEL // ck
    per_seq = seq // tm

    def rows(cols):
        def index(s):
            t = jnp.maximum(s - nck, 0)
            return ((t // per_seq) * nb + 1 + (t % per_seq) * (tm // CHUNK)) * CHUNK, 0
        return pl.BlockSpec((pl.Element(tm), pl.Element(cols)), index)

    return pl.pallas_call(
        functools.partial(_proj_ln_kernel, nck, 1),
        out_shape=jax.ShapeDtypeStruct((B * seq, D_MODEL), F32),
        grid=(nck + B * per_seq,),
        in_specs=[
            rows(kdim),
            pl.BlockSpec((None, kdim, ck), lambda s: (layer, 0, jnp.minimum(s, nck - 1)),
                         pipeline_mode=pl.Buffered(1)),
            rows(D_MODEL),
            pl.BlockSpec((None, 1, D_MODEL), lambda s: (layer, 0, 0)),
            pl.BlockSpec((None, 1, D_MODEL), lambda s: (layer, 0, 0)),
        ],
        out_specs=pl.BlockSpec((tm, D_MODEL), lambda s: (jnp.maximum(s - nck, 0), 0)),
        scratch_shapes=[pltpu.VMEM((kdim, D_MODEL), BF16)],
        compiler_params=_params("arbitrary"),
        name=name,
    )(a, w, hf, g.reshape(DEPTH, 1, D_MODEL), b.reshape(DEPTH, 1, D_MODEL))


def _ffn_up_kernel(rb, overlap, h_ref, wg_ref, wv_ref, cwg_ref, cwv_ref, cbg_ref, cbv_ref, o_ref,
                   wgs_ref, wvs_ref):
    tn = o_ref.shape[1]
    last_col_tile = pl.program_id(0) == pl.num_programs(0) - 1

    def placed(w):
        moved = jnp.concatenate([w[:, overlap:], jnp.zeros((w.shape[0], overlap), w.dtype)], axis=1)
        return jnp.where(last_col_tile, moved, w)

    @pl.when(pl.program_id(1) == 0)
    def _():
        wgs_ref[...] = placed(wg_ref[...].astype(BF16))
        wvs_ref[...] = placed(wv_ref[...].astype(BF16))

    row8 = lax.broadcasted_iota(jnp.int32, (HALO, tn), 0)
    halves = ((wgs_ref, cwg_ref, cbg_ref), (wvs_ref, cwv_ref, cbv_ref))
    prevs = [jnp.zeros((HALO, tn), F32)] * 2
    gates = {}

    n_rows = h_ref.shape[0]
    stages = []
    for r0 in range(0, n_rows, rb):
        stages.append((r0, r0 + rb, 0))
        if r0 + rb < n_rows:
            stages.append((r0, r0 + rb, 1))
        else:
            stages += [(r0, r0 + rb // 2, 1), (r0 + rb // 2, r0 + rb, 1)]

    def matmul(i):
        r0, r1, half = stages[i]
        u = jnp.dot(h_ref[r0:r1, :], halves[half][0][...], preferred_element_type=F32)
        if r0 == 0:
            u = jnp.concatenate([jnp.zeros((PAD, tn), F32), u[PAD:]], axis=0)
        return u

    def shifted(u, prev, k):
        rolled = pltpu.roll(u, k, 0)
        head = rolled[:HALO]
        for t in range(k):
            head = jnp.where(row8 == t, prev[HALO - k + t:HALO - k + t + 1], head)
        return jnp.concatenate([head, rolled[HALO:]], axis=0)

    def conv_geglu(i, u):
        r0, r1, half = stages[i]
        _, cw_ref, cb_ref = halves[half]
        prev = prevs[half]
        prevs[half] = u[r1 - r0 - HALO:r1 - r0, :]
        scale = 1.0 if half == 0 else 0.5
        cw = placed(cw_ref[...]) * scale
        c = (shifted(u, prev, 2) * cw[0:1] + shifted(u, prev, 1) * cw[1:2] + u * cw[2:3]
             + placed(cb_ref[...]) * scale)
        if half == 0:
            gates[r0] = c
        else:
            g0 = (r0 // rb) * rb
            gate = gates[g0][r0 - g0:r1 - g0, :]
            o_ref[r0:r1, :] = (gate * (1.0 + lax.erf(gate * (2.0 ** -0.5))) * c).astype(BF16)

    _software_pipeline(len(stages), matmul, conv_geglu)


def _ffn_up(hb, w_up, conv_w, conv_b, layer, L):
    T = hb.shape[0]
    tm, tn = L, TN_FF
    assert PAD <= TM_BIG and L % TM_BIG == 0
    nt = pl.cdiv(D_FF, tn)
    overlap = nt * tn - D_FF
    lane_tile = lambda n: jnp.minimum(n * (tn // LANES), (D_FF - tn) // LANES)

    def cols(rows, half):
        return pl.BlockSpec((None, pl.Element(rows), pl.Element(tn)),
                            lambda n, m: (layer, 0, (lane_tile(n) + half * (D_FF // LANES)) * LANES))

    return pl.pallas_call(
        functools.partial(_ffn_up_kernel, TM_BIG, overlap),
        out_shape=jax.ShapeDtypeStruct((T, nt * tn), BF16),
        grid=(nt, T // tm),
        in_specs=[
            pl.BlockSpec((tm, D_MODEL), lambda n, m: (m, 0)),
            cols(D_MODEL, 0), cols(D_MODEL, 1),
            cols(CONV_W, 0), cols(CONV_W, 1),
            cols(1, 0), cols(1, 1),
        ],
        out_specs=pl.BlockSpec((tm, tn), lambda n, m: (m, n)),
        scratch_shapes=[pltpu.VMEM((D_MODEL, tn), BF16), pltpu.VMEM((D_MODEL, tn), BF16)],
        compiler_params=_params("arbitrary", "arbitrary"),
        name="ffn_up_conv_geglu",
    )(hb, w_up, w_up, conv_w, conv_w, conv_b, conv_b)


def _const(a):
    return jnp.asarray(np.ascontiguousarray(a, dtype=np.float32))


def _rotary_tables(L):
    pos = (np.arange(L) - PAD).astype(np.float64)
    a_freq = 1.0 / (ROPE_THETA ** (np.arange(0, A_HEAD_DIM, 2, dtype=np.float64) / A_HEAD_DIM))
    ang = pos[:, None] * a_freq[None, :]
    ang = np.concatenate([ang] * (LANES // (A_HEAD_DIM // 2)), axis=-1)
    lane = np.arange(LANES)
    sign_a = np.where((lane & (A_HEAD_DIM // 2)) == 0, -1.0, 1.0)
    cos_a, sin_a = np.cos(ang), np.sin(ang) * sign_a
    scale = (A_HEAD_DIM ** -0.5) * math.log2(math.e)
    cos_a = np.stack([cos_a * scale, cos_a])
    sin_a = np.stack([sin_a * scale, sin_a])

    r_freq = 1.0 / (ROPE_THETA ** np.linspace(0.0, 1.0, R_QK_DIM // 2, dtype=np.float64))
    ang = pos[:, None] * r_freq[None, :]
    ang = np.concatenate([ang, ang], axis=-1)
    sign_r = np.where(lane < R_QK_DIM // 2, -1.0, 1.0)
    cos_r = np.broadcast_to(np.cos(ang), (2, L, LANES))
    sin_r = np.broadcast_to(np.sin(ang) * sign_r, (2, L, LANES))
    valid = (np.arange(L) >= PAD).astype(np.float64)
    post_k = np.broadcast_to(((R_QK_DIM ** -0.5) * valid)[:, None], (L, LANES))
    post_r = np.stack([np.ones((L, LANES)), post_k])
    return (_const(cos_a), _const(sin_a)), (_const(cos_r), _const(sin_r), _const(post_r))


def _retention_tables():
    log_gamma = np.log(1.0 - 2.0 ** (-5.0 - np.arange(R_HEADS, dtype=np.float64)))
    p = np.arange(CHUNK, dtype=np.float64)
    diff = p[:, None] - p[None, :]
    lg = log_gamma[:, None, None]
    decay = np.where(diff >= 0, np.exp(lg * np.maximum(diff, 0.0)), 0.0)
    tail = np.exp(log_gamma[:, None] * (CHUNK - 1 - p))
    qdec = np.exp(log_gamma[:, None] * (p + 1.0))
    tail = np.broadcast_to(tail[:, :, None], (R_HEADS, CHUNK, R_QK_DIM))
    qdec = np.broadcast_to(qdec[:, :, None], (R_HEADS, CHUNK, R_QK_DIM))
    cdec = np.broadcast_to(np.exp(log_gamma * CHUNK)[:, None, None], (R_HEADS, 1, R_V_DIM))
    return _const(decay), _const(tail), _const(qdec), _const(cdec)


def kernel(x, meta_tokens, ln_emb_g, ln_emb_b, w_in, lam_q1, lam_k1, lam_q2, lam_k2, subln_g, ret_gn_g,
           w_branch_a, w_branch_b, w_out, ln1_g, ln1_b, w_up, conv_w, conv_b, w_down, ln2_g, ln2_b):
    B, seq, _ = x.shape
    assert seq % CHUNK == 0 and x.shape[2] == D_MODEL and meta_tokens.shape == (N_META, D_MODEL)
    assert w_in.shape == (DEPTH, D_MODEL, W_IN_COLS) and w_up.shape == (DEPTH, D_MODEL, 2 * D_FF)
    nb = 1 + seq // CHUNK
    L = nb * CHUNK
    rot_a, rot_r = _rotary_tables(L)
    ret_tabs = _retention_tables()

    conv_b3 = conv_b.reshape(DEPTH, 1, 2 * D_FF)

    hf, hb = _embed_ln(x, meta_tokens, ln_emb_g, ln_emb_b, nb)
    for l in range(DEPTH):
        qk = _proj(hb, w_in, l, C_AQ, 2 * A_QK, "rot_a", BF16, rot_a, lambda n: n // (A_QK // TN_PROJ))
        av = _proj(hb, w_in, l, C_AV, A_VW, "plain", BF16)
        rqk = _proj(hb, w_in, l, C_RQ, 2 * R_QK, "rot_r", F32, rot_r, lambda n: n // (R_QK // TN_PROJ))
        rv = _proj(hb, w_in, l, C_RV, R_VW, "plain", BF16)
        gates = _proj(hb, w_in, l, C_RG, R_VW + 2 * D_MODEL, "plain", F32)

        lamp = jnp.stack([lam_q1[l], lam_k1[l], lam_q2[l], lam_k2[l]])
        oa = _attention(qk, av, lamp, subln_g[l], l, B, L)
        ob = _retention(rqk, rv, gates, ret_gn_g[l], ret_tabs, B, L)
        merged = _merge(oa, ob, w_branch_a, w_branch_b, gates, l)
        hf, hb = _proj_ln(merged, D_MODEL, w_out, l, hf, ln1_g, ln1_b, TM_SMALL, STAGE_COLS, "out_proj_ln")

        gact = _ffn_up(hb, w_up, conv_w, conv_b3, l, L)
        if l + 1 < DEPTH:
            hf, hb = _proj_ln(gact, D_FF, w_down, l, hf, ln2_g, ln2_b, TM_RESIDENT, STAGE_COLS_DOWN, "ffn_down_ln")
        else:
            out = _proj_ln_final(gact, D_FF, w_down, l, hf, ln2_g, ln2_b, TM_FINAL, STAGE_COLS_DOWN, nb,
                                 "ffn_down_ln_out")
    return out.reshape(B, seq, D_MODEL)
```
